```python
import math
import jax, jax.numpy as jnp
from jax import lax
import numpy as np

D_MODEL = 1024
BATCH = 4
SEQ = 8192
DEPTH = 1

HEAD_DIM = 64
RET_HEADS = 8
RET_WIDTH = RET_HEADS * HEAD_DIM
DIFF_HEADS = 4
DIFF_VDIM = 2 * HEAD_DIM
DIFF_WIDTH = DIFF_HEADS * DIFF_VDIM
MIX_WIDTH = RET_WIDTH + DIFF_WIDTH
DIFF_QK_WIDTH = 2 * DIFF_HEADS * HEAD_DIM
IN_WIDTH = 4 * RET_WIDTH + 2 * DIFF_QK_WIDTH + DIFF_WIDTH
D_FF = 2816
CHUNK = 128
Q_BLOCK = 128
ROPE_THETA = 10000.0
RET_THETA = 10000.0
LN_EPS = 1e-5
NORM_EPS = 1e-6
DEEPNORM_ALPHA = (2.0 * DEPTH) ** 0.25
DEEPNORM_BETA = (8.0 * DEPTH) ** -0.25

kernel_name = 'hybrid_retention_diffattn_macaron_deepnorm'

F32 = jnp.float32


def layer_norm(x, w, b):
    xf = x.astype(F32)
    mu = xf.mean(-1, keepdims=True)
    var = jnp.square(xf - mu).mean(-1, keepdims=True)
    y = (xf - mu) * lax.rsqrt(var + LN_EPS) * w.astype(F32) + b.astype(F32)
    return y.astype(x.dtype)


def swiglu(x, w_gate, w_up, w_down):
    return (jax.nn.silu(x @ w_gate) * (x @ w_up)) @ w_down


def rotary(x, positions):
    d = x.shape[-1]
    inv = 1.0 / (ROPE_THETA ** (jnp.arange(0, d, 2, dtype=F32) / d))
    ang = positions.astype(F32)[..., None] * inv
    c = jnp.cos(ang)[:, :, None, :]
    s = jnp.sin(ang)[:, :, None, :]
    x1, x2 = jnp.split(x.astype(F32), 2, axis=-1)
    return jnp.concatenate([x1 * c - x2 * s, x1 * s + x2 * c], axis=-1)


def retention_rotate(x, positions):
    d = x.shape[-1]
    inv = 1.0 / (RET_THETA ** jnp.linspace(0.0, 1.0, d // 2, dtype=F32))
    ang = positions.astype(F32)[..., None] * inv
    c = jnp.cos(ang)[:, :, None, :]
    s = jnp.sin(ang)[:, :, None, :]
    xf = x.astype(F32)
    xe, xo = xf[..., 0::2], xf[..., 1::2]
    return jnp.stack([xe * c - xo * s, xo * c + xe * s], axis=-1).reshape(xf.shape)


def retention_chunkwise(q, k, v):
    B, S, H, d = q.shape
    n_chunks = S // CHUNK
    log_g = jnp.log(1.0 - 2.0 ** (-5.0 - jnp.arange(H, dtype=F32)))
    idx = jnp.arange(CHUNK, dtype=F32)
    rel = idx[:, None] - idx[None, :]
    intra_decay = jnp.where(rel >= 0, jnp.exp(log_g[:, None, None] * jnp.maximum(rel, 0.0)), 0.0)
    q_decay = jnp.exp(log_g[:, None] * (idx + 1.0))
    k_decay = jnp.exp(log_g[:, None] * (CHUNK - 1.0 - idx))
    chunk_decay = jnp.exp(log_g * CHUNK)

    def to_chunks(t):
        return t.reshape(B, n_chunks, CHUNK, H, d).transpose(1, 0, 3, 2, 4)

    def step(state, inp):
        qi, ki, vi = inp
        scores = jnp.einsum('bhid,bhjd->bhij', qi, ki) * intra_decay
        intra = jnp.einsum('bhij,bhje->bhie', scores, vi)
        cross = jnp.einsum('bhid,bhde->bhie', qi, state) * q_decay[None, :, :, None]
        new_state = state * chunk_decay[None, :, None, None] + jnp.einsum('bhjd,hj,bhje->bhde', ki, k_decay, vi)
        return new_state, intra + cross

    state0 = jnp.zeros((B, H, d, d), F32)
    _, out = lax.scan(step, state0, (to_chunks(q), to_chunks(k), to_chunks(v)))
    return out.transpose(1, 0, 3, 2, 4).reshape(B, S, H, d)


def diff_attention(q, k, v, lam):
    B, S, H2, d = q.shape
    H = H2 // 2
    n_blk = S // Q_BLOCK
    scale = d ** -0.5
    key_pos = jnp.arange(S)
    qb = q.reshape(B, n_blk, Q_BLOCK, H2, d).transpose(1, 0, 3, 2, 4)

    def block(args):
        qi, start = args
        s = jnp.einsum('bhqd,bkhd->bhqk', qi, k) * scale
        qpos = start + jnp.arange(Q_BLOCK)
        s = jnp.where(key_pos[None, :] <= qpos[:, None], s, -jnp.inf)
        p = jax.nn.softmax(s, axis=-1).reshape(B, H, 2, Q_BLOCK, S)
        a = p[:, :, 0] - lam * p[:, :, 1]
        return jnp.einsum('bhqk,bkhe->bqhe', a, v)

    out = lax.map(block, (qb, jnp.arange(n_blk) * Q_BLOCK))
    return out.transpose(1, 0, 2, 3, 4).reshape(B, S, H, 2 * d)


def head_layernorm(y, w):
    B, S, H, e = y.shape
    mu = y.mean(-1, keepdims=True)
    var = jnp.square(y - mu).mean(-1, keepdims=True)
    return ((y - mu) * lax.rsqrt(var + NORM_EPS)).reshape(B, S, H * e) * w.astype(F32)


def head_rmsnorm(y, w):
    B, S, H, e = y.shape
    yn = y * lax.rsqrt(jnp.square(y).mean(-1, keepdims=True) + NORM_EPS)
    return yn.reshape(B, S, H * e) * w.astype(F32)


def token_mixer(x, positions, w_in, ret_norm_w, lq1, lk1, lq2, lk2, diff_norm_w, w_out, lambda_init):
    B, S, _ = x.shape
    h = x @ w_in
    splits = np.cumsum([RET_WIDTH] * 4 + [DIFF_QK_WIDTH] * 2)
    rq, rk, rv, rg, dq, dk, dv = jnp.split(h, splits, axis=-1)
    rq = retention_rotate(rq.reshape(B, S, RET_HEADS, HEAD_DIM), positions)
    rk = retention_rotate(rk.reshape(B, S, RET_HEADS, HEAD_DIM), positions) * (HEAD_DIM ** -0.5)
    rv = rv.reshape(B, S, RET_HEADS, HEAD_DIM).astype(F32)
    ret = retention_chunkwise(rq, rk, rv)
    ret = jax.nn.silu(rg.astype(F32)) * head_layernorm(ret, ret_norm_w)
    dq = rotary(dq.reshape(B, S, 2 * DIFF_HEADS, HEAD_DIM), positions)
    dk = rotary(dk.reshape(B, S, 2 * DIFF_HEADS, HEAD_DIM), positions)
    dv = dv.reshape(B, S, DIFF_HEADS, DIFF_VDIM).astype(F32)
    lam = (jnp.exp(jnp.sum(lq1.astype(F32) * lk1.astype(F32)))
           - jnp.exp(jnp.sum(lq2.astype(F32) * lk2.astype(F32))) + lambda_init)
    dif = diff_attention(dq, dk, dv, lam)
    dif = head_rmsnorm(dif, diff_norm_w) * (1.0 - lambda_init)
    merged = jnp.concatenate([ret, dif], axis=-1).astype(x.dtype)
    return merged @ w_out


def setup_inputs(seed: int = 0) -> dict:
    key = jax.random.key(seed)
    ks = jax.random.split(key, 24)
    L = DEPTH

    def nrm(k, shape, scale):
        return jax.random.normal(k, shape, F32) * scale

    def gain(k, n):
        return 1.0 + 0.02 * jax.random.normal(k, (L, n), F32)

    beta = DEEPNORM_BETA
    x = jax.random.normal(ks[0], (BATCH, SEQ, D_MODEL), F32)
    positions = jnp.broadcast_to(jnp.arange(SEQ, dtype=jnp.int32), (BATCH, SEQ))
    col_scale = np.ones((IN_WIDTH,), np.float32)
    col_scale[2 * RET_WIDTH:3 * RET_WIDTH] = beta
    col_scale[4 * RET_WIDTH + 2 * DIFF_QK_WIDTH:] = beta
    w_in = nrm(ks[1], (L, D_MODEL, IN_WIDTH), D_MODEL ** -0.5) * jnp.asarray(col_scale)
    return {
        'x': x,
        'positions': positions,
        'ffn1_w_gate': nrm(ks[2], (L, D_MODEL, D_FF), beta * D_MODEL ** -0.5),
        'ffn1_w_up': nrm(ks[3], (L, D_MODEL, D_FF), beta * D_MODEL ** -0.5),
        'ffn1_w_down': nrm(ks[4], (L, D_FF, D_MODEL), beta * D_FF ** -0.5),
        'ln1_w': gain(ks[5], D_MODEL),
        'ln1_b': nrm(ks[6], (L, D_MODEL), 0.02),
        'w_in': w_in,
        'ret_norm_w': gain(ks[7], RET_WIDTH),
        'diff_lambda_q1': nrm(ks[8], (L, HEAD_DIM), 0.1),
        'diff_lambda_k1': nrm(ks[9], (L, HEAD_DIM), 0.1),
        'diff_lambda_q2': nrm(ks[10], (L, HEAD_DIM), 0.1),
        'diff_lambda_k2': nrm(ks[11], (L, HEAD_DIM), 0.1),
        'diff_norm_w': gain(ks[12], DIFF_WIDTH),
        'w_out': nrm(ks[13], (L, MIX_WIDTH, D_MODEL), beta * MIX_WIDTH ** -0.5),
        'ln2_w': gain(ks[14], D_MODEL),
        'ln2_b': nrm(ks[15], (L, D_MODEL), 0.02),
        'ffn2_w_gate': nrm(ks[16], (L, D_MODEL, D_FF), beta * D_MODEL ** -0.5),
        'ffn2_w_up': nrm(ks[17], (L, D_MODEL, D_FF), beta * D_MODEL ** -0.5),
        'ffn2_w_down': nrm(ks[18], (L, D_FF, D_MODEL), beta * D_FF ** -0.5),
        'ln3_w': gain(ks[19], D_MODEL),
        'ln3_b': nrm(ks[20], (L, D_MODEL), 0.02),
    }


def reference(x, positions, ffn1_w_gate, ffn1_w_up, ffn1_w_down, ln1_w, ln1_b,
              w_in, ret_norm_w, diff_lambda_q1, diff_lambda_k1, diff_lambda_q2, diff_lambda_k2,
              diff_norm_w, w_out, ln2_w, ln2_b, ffn2_w_gate, ffn2_w_up, ffn2_w_down, ln3_w, ln3_b):
    alpha = DEEPNORM_ALPHA
    for l in range(DEPTH):
        lambda_init = 0.8 - 0.6 * math.exp(-0.3 * l)
        x = layer_norm(alpha * x + 0.5 * swiglu(x, ffn1_w_gate[l], ffn1_w_up[l], ffn1_w_down[l]), ln1_w[l], ln1_b[l])
        mix = token_mixer(x, positions, w_in[l], ret_norm_w[l], diff_lambda_q1[l], diff_lambda_k1[l],
                          diff_lambda_q2[l], diff_lambda_k2[l], diff_norm_w[l], w_out[l], lambda_init)
        x = layer_norm(alpha * x + mix, ln2_w[l], ln2_b[l])
        x = layer_norm(alpha * x + 0.5 * swiglu(x, ffn2_w_gate[l], ffn2_w_up[l], ffn2_w_down[l]), ln3_w[l], ln3_b[l])
    return x
```

```python
import functools
import math

import numpy as np
import jax
import jax.numpy as jnp
from jax import lax
from jax.experimental import pallas as pl
from jax.experimental.pallas import tpu as pltpu

F32 = jnp.float32
BF16 = jnp.bfloat16

D_MODEL = 1024
D_FF = 2816
HEAD_DIM = 64
RET_HEADS = 8
RET_WIDTH = RET_HEADS * HEAD_DIM
DIFF_HEADS = 4
DIFF_VDIM = 2 * HEAD_DIM
DIFF_WIDTH = DIFF_HEADS * DIFF_VDIM
GROUP_WIDTH = 512
N_GROUPS = 7
CHUNK = 128
ROPE_THETA = 10000.0
RET_THETA = 10000.0
LN_EPS = 1e-5
NORM_EPS = 1e-6
DEPTH = 1
DEEPNORM_ALPHA = (2.0 * DEPTH) ** 0.25
LAMBDA_INIT = 0.8 - 0.6 * math.exp(-0.3 * 0)

LANES = 128
VMEM_LIMIT = 56 * 1024 * 1024

TM_FFN = 512
FF_CHUNK = 1408
TM_PROJ = 512
TS_RET = 1024
TQ_ATT = 512


def _dot(a, b):
    return jnp.dot(a, b, preferred_element_type=F32)


def _dot_nt(a, b):
    return lax.dot_general(a, b, (((1,), (1,)), ((), ())), preferred_element_type=F32)


def _dot_tn(a, b):
    return lax.dot_general(a, b, (((0,), (0,)), ((), ())), preferred_element_type=F32)


def _layer_norm(z, w, b):
    mu = jnp.mean(z, axis=-1, keepdims=True)
    d = z - mu
    var = jnp.mean(d * d, axis=-1, keepdims=True)
    return d * lax.rsqrt(var + LN_EPS) * w + b


def _resident(shape):
    nd = len(shape)
    return pl.BlockSpec(shape, lambda *_: (0,) * nd, pipeline_mode=pl.Buffered(1))


def _ffn_ln_body(x_ref, wg_ref, wu_ref, wd_ref, lnw_ref, lnb_ref, o_ref):
    x = x_ref[...]
    xb = x.astype(BF16)
    y = jnp.zeros(x.shape, F32)
    for c in range(D_FF // FF_CHUNK):
        sl = slice(c * FF_CHUNK, (c + 1) * FF_CHUNK)
        g = _dot(xb, wg_ref[:, sl])
        u = _dot(xb, wu_ref[:, sl])
        h = (g * jax.nn.sigmoid(g) * u).astype(BF16)
        y = y + _dot(h, wd_ref[sl, :])
    z = DEEPNORM_ALPHA * x + 0.5 * y
    o_ref[...] = _layer_norm(z, lnw_ref[...], lnb_ref[...])


def _ffn_ln(x, wg, wu, wd, lnw, lnb):
    n = x.shape[0]
    tm = TM_FFN
    return pl.pallas_call(
        _ffn_ln_body,
        grid=(n // tm,),
        in_specs=[
            pl.BlockSpec((tm, D_MODEL), lambda i: (i, 0)),
            _resident((D_MODEL, D_FF)),
            _resident((D_MODEL, D_FF)),
            _resident((D_FF, D_MODEL)),
            _resident((1, D_MODEL)),
            _resident((1, D_MODEL)),
        ],
        out_specs=pl.BlockSpec((tm, D_MODEL), lambda i: (i, 0)),
        out_shape=jax.ShapeDtypeStruct((n, D_MODEL), F32),
        compiler_params=pltpu.CompilerParams(
            dimension_semantics=("arbitrary",), vmem_limit_bytes=VMEM_LIMIT),
        name="ffn_ln",
    )(x, wg, wu, wd, lnw, lnb)


def _in_proj_body(x_ref, pos_ref, w_ref, inv_ref,
                  rq_ref, rk_ref, rv_ref, rg_ref, dq_ref, dk_ref, dv_ref):
    xb = x_ref[...].astype(BF16)
    pos = pos_ref[...].astype(F32)
    lane = lax.broadcasted_iota(jnp.int32, (1, LANES), 1)
    first_half = (lane % HEAD_DIM) < (HEAD_DIM // 2)

    def tables(inv):
        ang = pos * inv
        s = jnp.sin(ang)
        return jnp.cos(ang), jnp.where(first_half, -s, s)

    def rotate(h, cs, scale):
        c, s = cs
        outs = []
        for j in range(GROUP_WIDTH // LANES):
            hj = h[:, j * LANES:(j + 1) * LANES]
            swapped = jnp.where(first_half,
                                pltpu.roll(hj, LANES - HEAD_DIM // 2, 1),
                                pltpu.roll(hj, HEAD_DIM // 2, 1))
            r = hj * c + swapped * s
            if scale != 1.0:
                r = r * scale
            outs.append(r.astype(BF16))
        return jnp.concatenate(outs, axis=1)

    def group(g):
        return _dot(xb, w_ref[:, g * GROUP_WIDTH:(g + 1) * GROUP_WIDTH])

    ret_cs = tables(inv_ref[0:1, :])
    rq_ref[...] = rotate(group(0), ret_cs, 1.0)
    rk_ref[...] = rotate(group(1), ret_cs, HEAD_DIM ** -0.5)
    rv_ref[...] = group(2).astype(BF16)
    rg_ref[...] = group(3).astype(BF16)
    rope_cs = tables(inv_ref[1:2, :])
    dq_ref[...] = rotate(group(4), rope_cs, HEAD_DIM ** -0.5)
    dk_ref[...] = rotate(group(5), rope_cs, 1.0)
    dv_ref[...] = group(6).astype(BF16)


def _in_proj(x1, pos, w_in, inv):
    n = x1.shape[0]
    tm = TM_PROJ
    out_spec = pl.BlockSpec((tm, GROUP_WIDTH), lambda i: (i, 0))
    out_shape = jax.ShapeDtypeStruct((n, GROUP_WIDTH), BF16)
    return pl.pallas_call(
        _in_proj_body,
        grid=(n // tm,),
        in_specs=[
            pl.BlockSpec((tm, D_MODEL), lambda i: (i, 0)),
            pl.BlockSpec((tm, 1), lambda i: (i, 0)),
            _resident((D_MODEL, N_GROUPS * GROUP_WIDTH)),
            _resident((2, LANES)),
        ],
        out_specs=[out_spec] * N_GROUPS,
        out_shape=[out_shape] * N_GROUPS,
        compiler_params=pltpu.CompilerParams(
            dimension_semantics=("arbitrary",), vmem_limit_bytes=VMEM_LIMIT),
        name="in_proj",
    )(x1, pos, w_in, inv)


def _retention_body(q_ref, k_ref, v_ref, g_ref, nw_ref, intra_ref, qd_ref, kd_ref, cd_ref, avg_ref,
                    o_ref, state_ref):
    @pl.when(pl.program_id(1) == 0)
    def _():
        state_ref[...] = jnp.zeros(state_ref.shape, F32)

    lane = lax.broadcasted_iota(jnp.int32, (CHUNK, LANES), 1)
    row = lax.broadcasted_iota(jnp.int32, (CHUNK, LANES), 0)
    head_a = lane < HEAD_DIM
    same_head = (row < HEAD_DIM) == head_a
    avg = avg_ref[...]

    def head_mean(t):
        hi = t.astype(BF16)
        lo = (t - hi.astype(F32)).astype(BF16)
        return _dot(hi, avg) + _dot(lo, avg)

    def chunk(c, carry):
        rows = pl.ds(pl.multiple_of(c * CHUNK, CHUNK), CHUNK)
        for p in range(RET_HEADS // 2):
            cols = slice(p * LANES, (p + 1) * LANES)
            q = q_ref[0, rows, cols]
            k = k_ref[0, rows, cols]
            v = v_ref[0, rows, cols]
            zero = jnp.zeros_like(q)
            sa = _dot_nt(jnp.where(head_a, q, zero), k) * intra_ref[2 * p]
            sb = _dot_nt(jnp.where(head_a, zero, q), k) * intra_ref[2 * p + 1]
            intra = jnp.where(head_a, _dot(sa.astype(BF16), v), _dot(sb.astype(BF16), v))
            state = state_ref[p]
            cross = _dot(q, state.astype(BF16)) * qd_ref[p]
            kdec = (k.astype(F32) * kd_ref[p]).astype(BF16)
            kv = _dot_tn(kdec, v)
            state_ref[p] = state * cd_ref[p] + jnp.where(same_head, kv, 0.0)
            o = intra + cross
            d = o - head_mean(o)
            y = d * lax.rsqrt(head_mean(d * d) + NORM_EPS)
            gate = g_ref[0, rows, cols].astype(F32)
            out = gate * jax.nn.sigmoid(gate) * (y * nw_ref[:, cols])
            o_ref[0, rows, cols] = out.astype(BF16)
        return carry

    lax.fori_loop(0, q_ref.shape[1] // CHUNK, chunk, 0)


def _retention(rq, rk, rv, rg, nw, intra, qd, kd, cd, avg):
    b, s, _ = rq.shape
    ts = TS_RET
    seq_spec = pl.BlockSpec((1, ts, RET_WIDTH), lambda bi, si: (bi, si, 0))
    return pl.pallas_call(
        _retention_body,
        grid=(b, s // ts),
        in_specs=[seq_spec, seq_spec, seq_spec, seq_spec,
                  _resident((1, RET_WIDTH)),
                  _resident(intra.shape), _resident(qd.shape), _resident(kd.shape),
                  _resident(cd.shape), _resident(avg.shape)],
        out_specs=seq_spec,
        out_shape=jax.ShapeDtypeStruct((b, s, RET_WIDTH), BF16),
        scratch_shapes=[pltpu.VMEM((RET_HEADS // 2, LANES, LANES), F32)],
        compiler_params=pltpu.CompilerParams(
            dimension_semantics=("arbitrary", "arbitrary"), vmem_limit_bytes=VMEM_LIMIT),
        name="retention",
    )(rq, rk, rv, rg, nw, intra, qd, kd, cd, avg)


def _diff_attn_body(q_ref, k_ref, v_ref, lam_ref, nw_ref, o_ref):
    tq = q_ref.shape[1]
    i = pl.program_id(2)
    q = q_ref[0]
    lane = lax.broadcasted_iota(jnp.int32, (1, LANES), 1)
    zero = jnp.zeros_like(q)
    q1 = jnp.where(lane < HEAD_DIM, q, zero)
    q2 = jnp.where(lane < HEAD_DIM, zero, q)

    def update(s, v, m, l, acc):
        m_new = jnp.maximum(m, jnp.max(s, axis=-1, keepdims=True))
        a = jnp.exp(m - m_new)
        p = jnp.exp(s - m_new)
        l = a * l + jnp.sum(p, axis=-1, keepdims=True)
        acc = a * acc + _dot(p.astype(BF16), v)
        return m_new, l, acc

    def step(kb, carry, diagonal):
        m1, l1, a1, m2, l2, a2 = carry
        rows = pl.ds(pl.multiple_of(kb * tq, tq), tq)
        k = k_ref[0, rows, :]
        v = v_ref[0, rows, :]
        s1 = _dot_nt(q1, k)
        s2 = _dot_nt(q2, k)
        if diagonal:
            r = lax.broadcasted_iota(jnp.int32, (tq, tq), 0)
            c = lax.broadcasted_iota(jnp.int32, (tq, tq), 1)
            s1 = jnp.where(c <= r, s1, -jnp.inf)
            s2 = jnp.where(c <= r, s2, -jnp.inf)
        m1, l1, a1 = update(s1, v, m1, l1, a1)
        m2, l2, a2 = update(s2, v, m2, l2, a2)
        return m1, l1, a1, m2, l2, a2

    neg = jnp.full((tq, 1), -jnp.inf, F32)
    zl = jnp.zeros((tq, 1), F32)
    za = jnp.zeros((tq, LANES), F32)
    carry = lax.fori_loop(0, i, functools.partial(step, diagonal=False), (neg, zl, za, neg, zl, za))
    m1, l1, a1, m2, l2, a2 = step(i, carry, diagonal=True)

    lam_v = lam_ref[...]
    lam = (jnp.exp(jnp.sum(lam_v[0:1] * lam_v[1:2], axis=-1, keepdims=True))
           - jnp.exp(jnp.sum(lam_v[2:3] * lam_v[3:4], axis=-1, keepdims=True)) + LAMBDA_INIT)
    o = a1 / l1 - lam * (a2 / l2)
    o = o * lax.rsqrt(jnp.mean(o * o, axis=-1, keepdims=True) + NORM_EPS)
    o_ref[0] = (o * nw_ref[...] * (1.0 - LAMBDA_INIT)).astype(BF16)


def _diff_attn(dq, dk, dv, lam_vecs, nw):
    b, s, _ = dq.shape
    tq = TQ_ATT
    return pl.pallas_call(
        _diff_attn_body,
        grid=(b, DIFF_HEADS, s // tq),
        in_specs=[
            pl.BlockSpec((1, tq, LANES), lambda bi, j, i: (bi, i, j)),
            pl.BlockSpec((1, s, LANES), lambda bi, j, i: (bi, 0, j)),
            pl.BlockSpec((1, s, LANES), lambda bi, j, i: (bi, 0, j)),
            _resident((4, HEAD_DIM)),
            pl.BlockSpec((1, LANES), lambda bi, j, i: (0, j)),
        ],
        out_specs=pl.BlockSpec((1, tq, LANES), lambda bi, j, i: (bi, i, j)),
        out_shape=jax.ShapeDtypeStruct((b, s, DIFF_WIDTH), BF16),
        compiler_params=pltpu.CompilerParams(
            dimension_semantics=("arbitrary", "arbitrary", "arbitrary"),
            vmem_limit_bytes=VMEM_LIMIT),
        name="diff_attn",
    )(dq, dk, dv, lam_vecs, nw)


def _out_ln_body(x_ref, ret_ref, dif_ref, w_ref, lnw_ref, lnb_ref, o_ref):
    mix = _dot(ret_ref[...], w_ref[0:RET_WIDTH, :]) + _dot(dif_ref[...], w_ref[RET_WIDTH:, :])
    z = DEEPNORM_ALPHA * x_ref[...] + mix
    o_ref[...] = _layer_norm(z, lnw_ref[...], lnb_ref[...])


def _out_ln(x1, ret, dif, w_out, lnw, lnb):
    n = x1.shape[0]
    tm = TM_PROJ
    return pl.pallas_call(
        _out_ln_body,
        grid=(n // tm,),
        in_specs=[
            pl.BlockSpec((tm, D_MODEL), lambda i: (i, 0)),
            pl.BlockSpec((tm, RET_WIDTH), lambda i: (i, 0)),
            pl.BlockSpec((tm, DIFF_WIDTH), lambda i: (i, 0)),
            _resident((RET_WIDTH + DIFF_WIDTH, D_MODEL)),
            _resident((1, D_MODEL)),
            _resident((1, D_MODEL)),
        ],
        out_specs=pl.BlockSpec((tm, D_MODEL), lambda i: (i, 0)),
        out_shape=jax.ShapeDtypeStruct((n, D_MODEL), F32),
        compiler_params=pltpu.CompilerParams(
            dimension_semantics=("arbitrary",), vmem_limit_bytes=VMEM_LIMIT),
        name="out_ln",
    )(x1, ret, dif, w_out, lnw, lnb)


def _rotation_inv_freqs():
    half = HEAD_DIM // 2
    inv_ret = 1.0 / (RET_THETA ** jnp.linspace(0.0, 1.0, half, dtype=F32))
    inv_rope = 1.0 / (ROPE_THETA ** (jnp.arange(0, HEAD_DIM, 2, dtype=F32) / HEAD_DIM))
    reps = LANES // half
    return jnp.stack([jnp.tile(inv_ret, reps), jnp.tile(inv_rope, reps)])


def _retention_tables():
    h = RET_HEADS
    log_g = jnp.log(1.0 - 2.0 ** (-5.0 - jnp.arange(h, dtype=F32)))
    idx = jnp.arange(CHUNK, dtype=F32)
    rel = idx[:, None] - idx[None, :]
    intra = jnp.where(rel >= 0, jnp.exp(log_g[:, None, None] * jnp.maximum(rel, 0.0)), 0.0)
    q_decay = jnp.exp(log_g[:, None] * (idx + 1.0))
    k_decay = jnp.exp(log_g[:, None] * (CHUNK - 1.0 - idx))
    chunk_decay = jnp.exp(log_g * CHUNK)

    def slab(t):
        return jnp.repeat(t.reshape(h // 2, 2, CHUNK).transpose(0, 2, 1), HEAD_DIM, axis=2)

    cd = jnp.repeat(chunk_decay.reshape(h // 2, 2), HEAD_DIM, axis=1)
    cd = jnp.broadcast_to(cd[:, :, None], (h // 2, LANES, LANES))
    head_of = np.arange(LANES) // HEAD_DIM
    avg = jnp.asarray((head_of[:, None] == head_of[None, :]) / HEAD_DIM, BF16)
    return intra, slab(q_decay), slab(k_decay), cd, avg


def _even_odd_head_perm():
    within = np.concatenate([np.arange(0, HEAD_DIM, 2), np.arange(1, HEAD_DIM, 2)])
    return (np.arange(RET_HEADS)[:, None] * HEAD_DIM + within[None, :]).reshape(-1)


def kernel(x, positions, ffn1_w_gate, ffn1_w_up, ffn1_w_down, ln1_w, ln1_b, w_in, ret_norm_w,
           diff_lambda_q1, diff_lambda_k1, diff_lambda_q2, diff_lambda_k2, diff_norm_w, w_out,
           ln2_w, ln2_b, ffn2_w_gate, ffn2_w_up, ffn2_w_down, ln3_w, ln3_b):
    b, s, d = x.shape
    n = b * s
    l = 0
    xf = x.reshape(n, d)
    pos = positions.reshape(n, 1)

    perm = _even_odd_head_perm()
    w = w_in[l]
    w_perm = jnp.concatenate(
        [w[:, 0:RET_WIDTH][:, perm], w[:, RET_WIDTH:2 * RET_WIDTH][:, perm], w[:, 2 * RET_WIDTH:]], axis=1
    ).astype(BF16)
    inv = _rotation_inv_freqs()
    intra, qd, kd, cd, avg = _retention_tables()
    lam_vecs = jnp.stack([diff_lambda_q1[l], diff_lambda_k1[l], diff_lambda_q2[l], diff_lambda_k2[l]])

    x1 = _ffn_ln(xf, ffn1_w_gate[l].astype(BF16), ffn1_w_up[l].astype(BF16), ffn1_w_down[l].astype(BF16),
                 ln1_w[l][None], ln1_b[l][None])
    rq, rk, rv, rg, dq, dk, dv = _in_proj(x1, pos, w_perm, inv)
    seq = lambda t: t.reshape(b, s, GROUP_WIDTH)
    ret = _retention(seq(rq), seq(rk), seq(rv), seq(rg), ret_norm_w[l][None], intra, qd, kd, cd, avg)
    dif = _diff_attn(seq(dq), seq(dk), seq(dv), lam_vecs, diff_norm_w[l][None])
    x2 = _out_ln(x1, ret.reshape(n, RET_WIDTH), dif.reshape(n, DIFF_WIDTH), w_out[l].astype(BF16),
                 ln2_w[l][None], ln2_b[l][None])
    out = _ffn_ln(x2, ffn2_w_gate[l].astype(BF16), ffn2_w_up[l].astype(BF16), ffn2_w_down[l].astype(BF16),
                  ln3_w[l][None], ln3_b[l][None])
    return out.reshape(b, s, d)
```

```python
import functools
import math

import numpy as np
import jax
import jax.numpy as jnp
from jax import lax
from jax.experimental import pallas as pl
from jax.experimental.pallas import tpu as pltpu

F32 = jnp.float32
BF16 = jnp.bfloat16

D_MODEL = 1024
D_FF = 2816
HEAD_DIM = 64
RET_HEADS = 8
RET_WIDTH = RET_HEADS * HEAD_DIM
DIFF_HEADS = 4
DIFF_VDIM = 2 * HEAD_DIM
DIFF_WIDTH = DIFF_HEADS * DIFF_VDIM
GROUP_WIDTH = 512
N_GROUPS = 7
CHUNK = 128
ROPE_THETA = 10000.0
RET_THETA = 10000.0
LN_EPS = 1e-5
NORM_EPS = 1e-6
DEPTH = 1
DEEPNORM_ALPHA = (2.0 * DEPTH) ** 0.25
LAMBDA_INIT = 0.8 - 0.6 * math.exp(-0.3 * 0)
LOG2_E = math.log2(math.e)

LANES = 128
VMEM_LIMIT = 56 * 1024 * 1024

TM_FFN = 512
MXU_TILE = 256
FF_SPLIT = (D_FF // MXU_TILE // 2) * MXU_TILE
TM_PROJ = 512
TS_RET = 1024
ONES_ROWS = 16


def _dot(a, b):
    return jnp.dot(a, b, preferred_element_type=F32)


def _dot_nt(a, b):
    return lax.dot_general(a, b, (((1,), (1,)), ((), ())), preferred_element_type=F32)


def _dot_tn(a, b):
    return lax.dot_general(a, b, (((0,), (0,)), ((), ())), preferred_element_type=F32)


def _layer_norm(z, w, b):
    mu = jnp.mean(z, axis=-1, keepdims=True)
    d = z - mu
    var = jnp.mean(d * d, axis=-1, keepdims=True)
    return d * lax.rsqrt(var + LN_EPS) * w + b


def _resident(shape):
    nd = len(shape)
    return pl.BlockSpec(shape, lambda *_: (0,) * nd, pipeline_mode=pl.Buffered(1))


def _ffn_ln_body(x_ref, wg_ref, wu_ref, wd_ref, lnw_ref, lnb_ref, o_ref):
    x = x_ref[...]
    xb = x.astype(BF16)
    y = jnp.zeros(x.shape, F32)
    for sl in (slice(0, FF_SPLIT), slice(FF_SPLIT, D_FF)):
        g = _dot(xb, wg_ref[:, sl])
        u = _dot(xb, wu_ref[:, sl])
        h = (g * jax.nn.sigmoid(g) * u).astype(BF16)
        y = y + _dot(h, wd_ref[sl, :])
    z = DEEPNORM_ALPHA * x + 0.5 * y
    o_ref[...] = _layer_norm(z, lnw_ref[...], lnb_ref[...])


def _ffn_ln(x, wg, wu, wd, lnw, lnb):
    n = x.shape[0]
    tm = TM_FFN
    return pl.pallas_call(
        _ffn_ln_body,
        grid=(n // tm,),
        in_specs=[
            pl.BlockSpec((tm, D_MODEL), lambda i: (i, 0)),
            _resident((D_MODEL, D_FF)),
            _resident((D_MODEL, D_FF)),
            _resident((D_FF, D_MODEL)),
            _resident((1, D_MODEL)),
            _resident((1, D_MODEL)),
        ],
        out_specs=pl.BlockSpec((tm, D_MODEL), lambda i: (i, 0)),
        out_shape=jax.ShapeDtypeStruct((n, D_MODEL), F32),
        compiler_params=pltpu.CompilerParams(
            dimension_semantics=("arbitrary",), vmem_limit_bytes=VMEM_LIMIT),
        name="ffn_ln",
    )(x, wg, wu, wd, lnw, lnb)


def _in_proj_body(x_ref, pos_ref, w_ref, inv_ref,
                  rq_ref, rk_ref, rv_ref, rg_ref, dqt_ref, dk_ref, dvt_ref):
    xb = x_ref[0].astype(BF16)
    pos = pos_ref[0].astype(F32)
    lane = lax.broadcasted_iota(jnp.int32, (1, LANES), 1)
    first_half = (lane % HEAD_DIM) < (HEAD_DIM // 2)
    slabs = range(GROUP_WIDTH // LANES)

    def tables(inv):
        ang = pos * inv
        s = jnp.sin(ang)
        return jnp.cos(ang), jnp.where(first_half, -s, s)

    def rotate(h, cs, scale):
        c, s = cs
        outs = []
        for j in slabs:
            hj = h[:, j * LANES:(j + 1) * LANES]
            swapped = jnp.where(first_half,
                                pltpu.roll(hj, LANES - HEAD_DIM // 2, 1),
                                pltpu.roll(hj, HEAD_DIM // 2, 1))
            r = hj * c + swapped * s
            outs.append(r if scale == 1.0 else r * scale)
        return outs

    def group(g):
        return _dot(xb, w_ref[:, g * GROUP_WIDTH:(g + 1) * GROUP_WIDTH])

    def store(ref, outs):
        ref[0] = jnp.concatenate([o.astype(BF16) for o in outs], axis=1)

    def store_transposed(ref, outs):
        for j, o in enumerate(outs):
            ref[0, 0, j * LANES:(j + 1) * LANES, :] = o.T.astype(BF16)

    ret_cs = tables(inv_ref[0:1, :])
    store(rq_ref, rotate(group(0), ret_cs, 1.0))
    store(rk_ref, rotate(group(1), ret_cs, HEAD_DIM ** -0.5))
    rv_ref[0] = group(2).astype(BF16)
    rg_ref[0] = group(3).astype(BF16)
    rope_cs = tables(inv_ref[1:2, :])
    store_transposed(dqt_ref, rotate(group(4), rope_cs, HEAD_DIM ** -0.5 * LOG2_E))
    store(dk_ref, rotate(group(5), rope_cs, 1.0))
    dv = group(6)
    store_transposed(dvt_ref, [dv[:, j * LANES:(j + 1) * LANES] for j in slabs])


def _in_proj(x1, pos, w_in, inv):
    b, s, _ = x1.shape
    tm = TM_PROJ
    row_spec = pl.BlockSpec((1, tm, GROUP_WIDTH), lambda bi, i: (bi, i, 0))
    col_spec = pl.BlockSpec((1, 1, GROUP_WIDTH, tm), lambda bi, i: (bi, i, 0, 0))
    row_shape = jax.ShapeDtypeStruct((b, s, GROUP_WIDTH), BF16)
    col_shape = jax.ShapeDtypeStruct((b, s // tm, GROUP_WIDTH, tm), BF16)
    return pl.pallas_call(
        _in_proj_body,
        grid=(b, s // tm),
        in_specs=[
            pl.BlockSpec((1, tm, D_MODEL), lambda bi, i: (bi, i, 0)),
            pl.BlockSpec((1, tm, 1), lambda bi, i: (bi, i, 0)),
            _resident((D_MODEL, N_GROUPS * GROUP_WIDTH)),
            _resident((2, LANES)),
        ],
        out_specs=[row_spec, row_spec, row_spec, row_spec, col_spec, row_spec, col_spec],
        out_shape=[row_shape, row_shape, row_shape, row_shape, col_shape, row_shape, col_shape],
        compiler_params=pltpu.CompilerParams(
            dimension_semantics=("arbitrary", "arbitrary"), vmem_limit_bytes=VMEM_LIMIT),
        name="in_proj",
    )(x1, pos, w_in, inv)


def _retention_body(q_ref, k_ref, v_ref, g_ref, nw_ref, intra_ref, qd_ref, kd_ref, cd_ref, avg_ref,
                    o_ref, state_ref, raw_ref):
    @pl.when(pl.program_id(1) == 0)
    def _():
        state_ref[...] = jnp.zeros(state_ref.shape, F32)

    lane = lax.broadcasted_iota(jnp.int32, (CHUNK, LANES), 1)
    row = lax.broadcasted_iota(jnp.int32, (CHUNK, LANES), 0)
    head_a = lane < HEAD_DIM
    same_head = (row < HEAD_DIM) == head_a
    avg = avg_ref[...]
    pairs = [slice(p * LANES, (p + 1) * LANES) for p in range(RET_HEADS // 2)]

    for p, cols in enumerate(pairs):
        state = state_ref[p]
        for c in range(q_ref.shape[1] // CHUNK):
            rows = slice(c * CHUNK, (c + 1) * CHUNK)
            q = q_ref[0, rows, cols]
            k = k_ref[0, rows, cols]
            v = v_ref[0, rows, cols]
            zero = jnp.zeros_like(q)
            sa = _dot_nt(jnp.where(head_a, q, zero), k) * intra_ref[2 * p]
            sb = _dot_nt(jnp.where(head_a, zero, q), k) * intra_ref[2 * p + 1]
            intra = jnp.where(head_a, _dot(sa.astype(BF16), v), _dot(sb.astype(BF16), v))
            cross = _dot(q, state.astype(BF16)) * qd_ref[p]
            kdec = (k.astype(F32) * kd_ref[p]).astype(BF16)
            state = state * cd_ref[p] + jnp.where(same_head, _dot_tn(kdec, v), 0.0)
            raw_ref[rows, cols] = intra + cross
        state_ref[p] = state

    def head_mean(t):
        hi = t.astype(BF16)
        lo = (t - hi.astype(F32)).astype(BF16)
        return _dot(hi, avg) + _dot(lo, avg)

    for cols in pairs:
        o = raw_ref[:, cols]
        d = o - head_mean(o)
        y = d * lax.rsqrt(head_mean(d * d) + NORM_EPS)
        gate = g_ref[0, :, cols].astype(F32)
        o_ref[0, :, cols] = (gate * jax.nn.sigmoid(gate) * (y * nw_ref[:, cols])).astype(BF16)


def _retention(rq, rk, rv, rg, nw, intra, qd, kd, cd, avg):
    b, s, _ = rq.shape
    ts = TS_RET
    seq_spec = pl.BlockSpec((1, ts, RET_WIDTH), lambda bi, si: (bi, si, 0))
    return pl.pallas_call(
        _retention_body,
        grid=(b, s // ts),
        in_specs=[seq_spec, seq_spec, seq_spec, seq_spec,
                  _resident((1, RET_WIDTH)),
                  _resident(intra.shape), _resident(qd.shape), _resident(kd.shape),
                  _resident(cd.shape), _resident(avg.shape)],
        out_specs=seq_spec,
        out_shape=jax.ShapeDtypeStruct((b, s, RET_WIDTH), BF16),
        scratch_shapes=[pltpu.VMEM((RET_HEADS // 2, LANES, LANES), F32),
                        pltpu.VMEM((ts, RET_WIDTH), F32)],
        compiler_params=pltpu.CompilerParams(
            dimension_semantics=("arbitrary", "arbitrary"), vmem_limit_bytes=VMEM_LIMIT),
        name="retention",
    )(rq, rk, rv, rg, nw, intra, qd, kd, cd, avg)


def _diff_attn_body(qt_ref, k_ref, vt_ref, lam_ref, nw_ref, o_ref, s_ref, mx_ref, m_ref, acc_ref):
    tq = qt_ref.shape[3]
    i = pl.program_id(2)
    qt = qt_ref[0, 0]
    row = lax.broadcasted_iota(jnp.int32, (LANES, 1), 0)
    zero = jnp.zeros_like(qt)
    qts = (jnp.where(row < HEAD_DIM, qt, zero), jnp.where(row < HEAD_DIM, zero, qt))
    heads = range(2)

    m_ref[...] = jnp.full(m_ref.shape, -jnp.inf, F32)
    acc_ref[...] = jnp.zeros(acc_ref.shape, F32)
    ones = jnp.ones((ONES_ROWS, tq), BF16)

    def scores(slot, t):
        k = k_ref[0, pl.ds(pl.multiple_of(t * tq, tq), tq), :]
        for h in heads:
            st = _dot(k, qts[h])
            s_ref[slot, h] = st
            mx_ref[slot, h] = jnp.max(st, axis=0, keepdims=True)

    def attend(slot, t, diagonal):
        vt = jnp.concatenate([vt_ref[0, t], ones], axis=0)
        for h in heads:
            st = s_ref[slot, h]
            mx = mx_ref[slot, h]
            if diagonal:
                key = lax.broadcasted_iota(jnp.int32, (tq, tq), 0)
                qry = lax.broadcasted_iota(jnp.int32, (tq, tq), 1)
                st = jnp.where(key <= qry, st, -jnp.inf)
                mx = jnp.max(st, axis=0, keepdims=True)
            m_new = jnp.maximum(m_ref[h], mx)
            a = jnp.exp2(m_ref[h] - m_new)
            p = jnp.exp2(st - m_new)
            m_ref[h] = m_new
            acc_ref[h] = a * acc_ref[h] + _dot(vt, p.astype(BF16))

    scores(0, 0)

    def pair(j, carry):
        t = 2 * j
        scores(1, t + 1)
        attend(0, t, False)
        scores(0, t + 2)
        attend(1, t + 1, False)
        return carry

    lax.fori_loop(0, i // 2, pair, 0)

    @pl.when(i % 2 == 0)
    def _():
        attend(0, i, True)

    @pl.when(i % 2 == 1)
    def _():
        scores(1, i)
        attend(0, i - 1, False)
        attend(1, i, True)

    lam_v = lam_ref[...]
    lam = (jnp.exp(jnp.sum(lam_v[0:1] * lam_v[1:2], axis=-1, keepdims=True))
           - jnp.exp(jnp.sum(lam_v[2:3] * lam_v[3:4], axis=-1, keepdims=True)) + LAMBDA_INIT)
    num = [acc_ref[h, 0:LANES, :] for h in heads]
    den = [acc_ref[h, LANES:LANES + 1, :] for h in heads]
    o = (num[0] / den[0] - lam * (num[1] / den[1])).T
    o = o * lax.rsqrt(jnp.mean(o * o, axis=-1, keepdims=True) + NORM_EPS)
    o_ref[0] = (o * nw_ref[...] * (1.0 - LAMBDA_INIT)).astype(BF16)


def _diff_attn(dqt, dk, dvt, lam_vecs, nw):
    b, nblk, _, tq = dqt.shape
    s = nblk * tq
    return pl.pallas_call(
        _diff_attn_body,
        grid=(b, DIFF_HEADS, nblk),
        in_specs=[
            pl.BlockSpec((1, 1, LANES, tq), lambda bi, j, i: (bi, i, j, 0)),
            pl.BlockSpec((1, s, LANES), lambda bi, j, i: (bi, 0, j)),
            pl.BlockSpec((1, nblk, LANES, tq), lambda bi, j, i: (bi, 0, j, 0)),
            _resident((4, HEAD_DIM)),
            pl.BlockSpec((1, LANES), lambda bi, j, i: (0, j)),
        ],
        out_specs=pl.BlockSpec((1, tq, LANES), lambda bi, j, i: (bi, i, j)),
        out_shape=jax.ShapeDtypeStruct((b, s, DIFF_WIDTH), BF16),
        scratch_shapes=[pltpu.VMEM((2, 2, tq, tq), F32),
                        pltpu.VMEM((2, 2, 1, tq), F32),
                        pltpu.VMEM((2, 1, tq), F32),
                        pltpu.VMEM((2, LANES + ONES_ROWS, tq), F32)],
        compiler_params=pltpu.CompilerParams(
            dimension_semantics=("arbitrary", "arbitrary", "arbitrary"),
            vmem_limit_bytes=VMEM_LIMIT),
        name="diff_attn",
    )(dqt, dk, dvt, lam_vecs, nw)


def _out_ln_body(x_ref, ret_ref, dif_ref, w_ref, lnw_ref, lnb_ref, o_ref):
    mix = _dot(ret_ref[...], w_ref[0:RET_WIDTH, :]) + _dot(dif_ref[...], w_ref[RET_WIDTH:, :])
    z = DEEPNORM_ALPHA * x_ref[...] + mix
    o_ref[...] = _layer_norm(z, lnw_ref[...], lnb_ref[...])


def _out_ln(x1, ret, dif, w_out, lnw, lnb):
    n = x1.shape[0]
    tm = TM_PROJ
    return pl.pallas_call(
        _out_ln_body,
        grid=(n // tm,),
        in_specs=[
            pl.BlockSpec((tm, D_MODEL), lambda i: (i, 0)),
            pl.BlockSpec((tm, RET_WIDTH), lambda i: (i, 0)),
            pl.BlockSpec((tm, DIFF_WIDTH), lambda i: (i, 0)),
            _resident((RET_WIDTH + DIFF_WIDTH, D_MODEL)),
            _resident((1, D_MODEL)),
            _resident((1, D_MODEL)),
        ],
        out_specs=pl.BlockSpec((tm, D_MODEL), lambda i: (i, 0)),
        out_shape=jax.ShapeDtypeStruct((n, D_MODEL), F32),
        compiler_params=pltpu.CompilerParams(
            dimension_semantics=("arbitrary",), vmem_limit_bytes=VMEM_LIMIT),
        name="out_ln",
    )(x1, ret, dif, w_out, lnw, lnb)


def _rotation_inv_freqs():
    half = HEAD_DIM // 2
    inv_ret = 1.0 / (RET_THETA ** jnp.linspace(0.0, 1.0, half, dtype=F32))
    inv_rope = 1.0 / (ROPE_THETA ** (jnp.arange(0, HEAD_DIM, 2, dtype=F32) / HEAD_DIM))
    reps = LANES // half
    return jnp.stack([jnp.tile(inv_ret, reps), jnp.tile(inv_rope, reps)])


def _retention_tables():
    h = RET_HEADS
    log_g = jnp.log(1.0 - 2.0 ** (-5.0 - jnp.arange(h, dtype=F32)))
    idx = jnp.arange(CHUNK, dtype=F32)
    rel = idx[:, None] - idx[None, :]
    intra = jnp.where(rel >= 0, jnp.exp(log_g[:, None, None] * jnp.maximum(rel, 0.0)), 0.0)
    q_decay = jnp.exp(log_g[:, None] * (idx + 1.0))
    k_decay = jnp.exp(log_g[:, None] * (CHUNK - 1.0 - idx))
    chunk_decay = jnp.exp(log_g * CHUNK)

    def slab(t):
        return jnp.repeat(t.reshape(h // 2, 2, CHUNK).transpose(0, 2, 1), HEAD_DIM, axis=2)

    cd = jnp.repeat(chunk_decay.reshape(h // 2, 2), HEAD_DIM, axis=1)
    cd = jnp.broadcast_to(cd[:, :, None], (h // 2, LANES, LANES))
    head_of = np.arange(LANES) // HEAD_DIM
    avg = jnp.asarray((head_of[:, None] == head_of[None, :]) / HEAD_DIM, BF16)
    return intra, slab(q_decay), slab(k_decay), cd, avg


def _even_odd_head_perm():
    within = np.concatenate([np.arange(0, HEAD_DIM, 2), np.arange(1, HEAD_DIM, 2)])
    return (np.arange(RET_HEADS)[:, None] * HEAD_DIM + within[None, :]).reshape(-1)


def kernel(x, positions, ffn1_w_gate, ffn1_w_up, ffn1_w_down, ln1_w, ln1_b, w_in, ret_norm_w,
           diff_lambda_q1, diff_lambda_k1, diff_lambda_q2, diff_lambda_k2, diff_norm_w, w_out,
           ln2_w, ln2_b, ffn2_w_gate, ffn2_w_up, ffn2_w_down, ln3_w, ln3_b):
    b, s, d = x.shape
    n = b * s
    l = 0
    xf = x.reshape(n, d)
    pos = positions.reshape(b, s, 1)

    perm = _even_odd_head_perm()
    w = w_in[l]
    w_perm = jnp.concatenate(
        [w[:, 0:RET_WIDTH][:, perm], w[:, RET_WIDTH:2 * RET_WIDTH][:, perm], w[:, 2 * RET_WIDTH:]], axis=1
    ).astype(BF16)
    inv = _rotation_inv_freqs()
    intra, qd, kd, cd, avg = _retention_tables()
    lam_vecs = jnp.stack([diff_lambda_q1[l], diff_lambda_k1[l], diff_lambda_q2[l], diff_lambda_k2[l]])

    x1 = _ffn_ln(xf, ffn1_w_gate[l].astype(BF16), ffn1_w_up[l].astype(BF16), ffn1_w_down[l].astype(BF16),
                 ln1_w[l][None], ln1_b[l][None])
    rq, rk, rv, rg, dqt, dk, dvt = _in_proj(x1.reshape(b, s, d), pos, w_perm, inv)
    ret = _retention(rq, rk, rv, rg, ret_norm_w[l][None], intra, qd, kd, cd, avg)
    dif = _diff_attn(dqt, dk, dvt, lam_vecs, diff_norm_w[l][None])
    x2 = _out_ln(x1, ret.reshape(n, RET_WIDTH), dif.reshape(n, DIFF_WIDTH), w_out[l].astype(BF16),
                 ln2_w[l][None], ln2_b[l][None])
    out = _ffn_ln(x2, ffn2_w_gate[l].astype(BF16), ffn2_w_up[l].astype(BF16), ffn2_w_down[l].astype(BF16),
                  ln3_w[l][None], ln3_b[l][None])
    return out.reshape(b, s, d)
```

```python
import functools
import math

import numpy as np
import jax
import jax.numpy as jnp
from jax import lax
from jax.experimental import pallas as pl
from jax.experimental.pallas import tpu as pltpu

F32 = jnp.float32
BF16 = jnp.bfloat16

D_MODEL = 1024
D_FF = 2816
HEAD_DIM = 64
RET_HEADS = 8
RET_WIDTH = RET_HEADS * HEAD_DIM
DIFF_HEADS = 4
DIFF_VDIM = 2 * HEAD_DIM
DIFF_WIDTH = DIFF_HEADS * DIFF_VDIM
GROUP_WIDTH = 512
N_GROUPS = 7
CHUNK = 128
ROPE_THETA = 10000.0
RET_THETA = 10000.0
LN_EPS = 1e-5
NORM_EPS = 1e-6
DEPTH = 1
DEEPNORM_ALPHA = (2.0 * DEPTH) ** 0.25
LAMBDA_INIT = 0.8 - 0.6 * math.exp(-0.3 * 0)
LOG2_E = math.log2(math.e)

LANES = 128
VMEM_LIMIT = 56 * 1024 * 1024

TM_FFN = 512
MXU_TILE = 256
FF_SPLIT = (D_FF // MXU_TILE // 2) * MXU_TILE
TM_PROJ = 512
TS_RET = 1024
ONES_ROWS = 16


def _dot(a, b):
    return jnp.dot(a, b, preferred_element_type=F32)


def _dot_nt(a, b):
    return lax.dot_general(a, b, (((1,), (1,)), ((), ())), preferred_element_type=F32)


def _dot_tn(a, b):
    return lax.dot_general(a, b, (((0,), (0,)), ((), ())), preferred_element_type=F32)


def _layer_norm(z, w, b):
    mu = jnp.mean(z, axis=-1, keepdims=True)
    d = z - mu
    var = jnp.mean(d * d, axis=-1, keepdims=True)
    return d * lax.rsqrt(var + LN_EPS) * w + b


def _resident(shape):
    nd = len(shape)
    return pl.BlockSpec(shape, lambda *_: (0,) * nd, pipeline_mode=pl.Buffered(1))


def _ffn_ln_body(*refs, with_mixer):
    if with_mixer:
        x_ref, ret_ref, dif_ref, wo_ref, mlnw_ref, mlnb_ref, wg_ref, wu_ref, wd_ref, lnw_ref, lnb_ref, o_ref = refs
    else:
        x_ref, wg_ref, wu_ref, wd_ref, lnw_ref, lnb_ref, o_ref = refs
    half = x_ref.shape[0] // 2
    for rows in (slice(0, half), slice(half, 2 * half)):
        x = x_ref[rows, :]
        if with_mixer:
            mix = _dot(ret_ref[rows, :], wo_ref[0:RET_WIDTH, :]) + _dot(dif_ref[rows, :], wo_ref[RET_WIDTH:, :])
            x = _layer_norm(DEEPNORM_ALPHA * x + mix, mlnw_ref[...], mlnb_ref[...])
        xb = x.astype(BF16)
        y = jnp.zeros(x.shape, F32)
        for sl in (slice(0, FF_SPLIT), slice(FF_SPLIT, D_FF)):
            g = _dot(xb, wg_ref[:, sl])
            u = _dot(xb, wu_ref[:, sl])
            h = (g * jax.nn.sigmoid(g) * u).astype(BF16)
            y = y + _dot(h, wd_ref[sl, :])
        z = DEEPNORM_ALPHA * x + 0.5 * y
        o_ref[rows, :] = _layer_norm(z, lnw_ref[...], lnb_ref[...])


def _ffn_ln(x, wg, wu, wd, lnw, lnb, mixer=None):
    n = x.shape[0]
    tm = TM_FFN
    rows = lambda width: pl.BlockSpec((tm, width), lambda i: (i, 0))
    vec = _resident((1, D_MODEL))
    operands, specs = [x], [rows(D_MODEL)]
    if mixer is not None:
        operands += list(mixer)
        specs += [rows(RET_WIDTH), rows(DIFF_WIDTH), _resident((RET_WIDTH + DIFF_WIDTH, D_MODEL)), vec, vec]
    operands += [wg, wu, wd, lnw, lnb]
    specs += [_resident((D_MODEL, D_FF)), _resident((D_MODEL, D_FF)), _resident((D_FF, D_MODEL)), vec, vec]
    return pl.pallas_call(
        functools.partial(_ffn_ln_body, with_mixer=mixer is not None),
        grid=(n // tm,),
        in_specs=specs,
        out_specs=rows(D_MODEL),
        out_shape=jax.ShapeDtypeStruct((n, D_MODEL), F32),
        compiler_params=pltpu.CompilerParams(
            dimension_semantics=("arbitrary",), vmem_limit_bytes=VMEM_LIMIT),
        name="mix_ffn_ln" if mixer is not None else "ffn_ln",
    )(*operands)


def _in_proj_body(x_ref, pos_ref, w_ref, inv_ref,
                  rq_ref, rk_ref, rv_ref, rg_ref, dqt_ref, dk_ref, dvt_ref):
    xb = x_ref[0].astype(BF16)
    pos = pos_ref[0].astype(F32)
    lane = lax.broadcasted_iota(jnp.int32, (1, LANES), 1)
    first_half = (lane % HEAD_DIM) < (HEAD_DIM // 2)
    slabs = range(GROUP_WIDTH // LANES)

    def tables(inv):
        ang = pos * inv
        s = jnp.sin(ang)
        return jnp.cos(ang), jnp.where(first_half, -s, s)

    def rotate(h, cs, scale):
        c, s = cs
        outs = []
        for j in slabs:
            hj = h[:, j * LANES:(j + 1) * LANES]
            swapped = jnp.where(first_half,
                                pltpu.roll(hj, LANES - HEAD_DIM // 2, 1),
                                pltpu.roll(hj, HEAD_DIM // 2, 1))
            r = hj * c + swapped * s
            outs.append(r if scale == 1.0 else r * scale)
        return outs

    def group(g):
        return _dot(xb, w_ref[:, g * GROUP_WIDTH:(g + 1) * GROUP_WIDTH])

    def store(ref, outs):
        ref[0] = jnp.concatenate([o.astype(BF16) for o in outs], axis=1)

    def store_transposed(ref, outs):
        for j, o in enumerate(outs):
            ref[0, 0, j * LANES:(j + 1) * LANES, :] = o.T.astype(BF16)

    ret_cs = tables(inv_ref[0:1, :])
    store(rq_ref, rotate(group(0), ret_cs, 1.0))
    store(rk_ref, rotate(group(1), ret_cs, HEAD_DIM ** -0.5))
    rv_ref[0] = group(2).astype(BF16)
    rg_ref[0] = group(3).astype(BF16)
    rope_cs = tables(inv_ref[1:2, :])
    store_transposed(dqt_ref, rotate(group(4), rope_cs, HEAD_DIM ** -0.5 * LOG2_E))
    store(dk_ref, rotate(group(5), rope_cs, 1.0))
    dv = group(6)
    store_transposed(dvt_ref, [dv[:, j * LANES:(j + 1) * LANES] for j in slabs])


def _in_proj(x1, pos, w_in, inv):
    b, s, _ = x1.shape
    tm = TM_PROJ
    row_spec = pl.BlockSpec((1, tm, GROUP_WIDTH), lambda bi, i: (bi, i, 0))
    col_spec = pl.BlockSpec((1, 1, GROUP_WIDTH, tm), lambda bi, i: (bi, i, 0, 0))
    row_shape = jax.ShapeDtypeStruct((b, s, GROUP_WIDTH), BF16)
    col_shape = jax.ShapeDtypeStruct((b, s // tm, GROUP_WIDTH, tm), BF16)
    return pl.pallas_call(
        _in_proj_body,
        grid=(b, s // tm),
        in_specs=[
            pl.BlockSpec((1, tm, D_MODEL), lambda bi, i: (bi, i, 0)),
            pl.BlockSpec((1, tm, 1), lambda bi, i: (bi, i, 0)),
            _resident((D_MODEL, N_GROUPS * GROUP_WIDTH)),
            _resident((2, LANES)),
        ],
        out_specs=[row_spec, row_spec, row_spec, row_spec, col_spec, row_spec, col_spec],
        out_shape=[row_shape, row_shape, row_shape, row_shape, col_shape, row_shape, col_shape],
        compiler_params=pltpu.CompilerParams(
            dimension_semantics=("arbitrary", "arbitrary"), vmem_limit_bytes=VMEM_LIMIT),
        name="in_proj",
    )(x1, pos, w_in, inv)


def _retention_body(q_ref, k_ref, v_ref, g_ref, nw_ref, intra_ref, qd_ref, kd_ref, cd_ref, avg_ref,
                    o_ref, state_ref, raw_ref):
    @pl.when(pl.program_id(1) == 0)
    def _():
        state_ref[...] = jnp.zeros(state_ref.shape, F32)

    lane = lax.broadcasted_iota(jnp.int32, (CHUNK, LANES), 1)
    row = lax.broadcasted_iota(jnp.int32, (CHUNK, LANES), 0)
    head_a = lane < HEAD_DIM
    same_head = (row < HEAD_DIM) == head_a
    avg = avg_ref[...]
    pairs = [slice(p * LANES, (p + 1) * LANES) for p in range(RET_HEADS // 2)]

    for p, cols in enumerate(pairs):
        state = state_ref[p]
        for c in range(q_ref.shape[1] // CHUNK):
            rows = slice(c * CHUNK, (c + 1) * CHUNK)
            q = q_ref[0, rows, cols]
            k = k_ref[0, rows, cols]
            v = v_ref[0, rows, cols]
            zero = jnp.zeros_like(q)
            sa = _dot_nt(jnp.where(head_a, q, zero), k) * intra_ref[2 * p]
            sb = _dot_nt(jnp.where(head_a, zero, q), k) * intra_ref[2 * p + 1]
            intra = jnp.where(head_a, _dot(sa.astype(BF16), v), _dot(sb.astype(BF16), v))
            cross = _dot(q, state.astype(BF16)) * qd_ref[p]
            kdec = (k.astype(F32) * kd_ref[p]).astype(BF16)
            state = state * cd_ref[p] + jnp.where(same_head, _dot_tn(kdec, v), 0.0)
            raw_ref[rows, cols] = intra + cross
        state_ref[p] = state

    def head_mean(t):
        hi = t.astype(BF16)
        lo = (t - hi.astype(F32)).astype(BF16)
        return _dot(jnp.concatenate([hi, lo], axis=1), avg)

    for cols in pairs:
        o = raw_ref[:, cols]
        d = o - head_mean(o)
        y = d * lax.rsqrt(head_mean(d * d) + NORM_EPS)
        gate = g_ref[0, :, cols].astype(F32)
        o_ref[0, :, cols] = (gate * jax.nn.sigmoid(gate) * (y * nw_ref[:, cols])).astype(BF16)


def _retention(rq, rk, rv, rg, nw, intra, qd, kd, cd, avg):
    b, s, _ = rq.shape
    ts = TS_RET
    seq_spec = pl.BlockSpec((1, ts, RET_WIDTH), lambda bi, si: (bi, si, 0))
    return pl.pallas_call(
        _retention_body,
        grid=(b, s // ts),
        in_specs=[seq_spec, seq_spec, seq_spec, seq_spec,
                  _resident((1, RET_WIDTH)),
                  _resident(intra.shape), _resident(qd.shape), _resident(kd.shape),
                  _resident(cd.shape), _resident(avg.shape)],
        out_specs=seq_spec,
        out_shape=jax.ShapeDtypeStruct((b, s, RET_WIDTH), BF16),
        scratch_shapes=[pltpu.VMEM((RET_HEADS // 2, LANES, LANES), F32),
                        pltpu.VMEM((ts, RET_WIDTH), F32)],
        compiler_params=pltpu.CompilerParams(
            dimension_semantics=("arbitrary", "arbitrary"), vmem_limit_bytes=VMEM_LIMIT),
        name="retention",
    )(rq, rk, rv, rg, nw, intra, qd, kd, cd, avg)


def _diff_attn_body(qt_ref, k_ref, vt_ref, lam_ref, nw_ref, o_ref, s0_ref, s1_ref, mx0_ref, mx1_ref, m_ref, acc_ref):
    nblk, _, tq = qt_ref.shape[1:]
    s_refs, mx_refs = (s0_ref, s1_ref), (mx0_ref, mx1_ref)
    row = lax.broadcasted_iota(jnp.int32, (LANES, 1), 0)
    first_head = row < HEAD_DIM
    heads = range(2)
    ones = jnp.ones((ONES_ROWS, tq), BF16)
    lam_v = lam_ref[...]
    lam = (jnp.exp(jnp.sum(lam_v[0:1] * lam_v[1:2], axis=-1, keepdims=True))
           - jnp.exp(jnp.sum(lam_v[2:3] * lam_v[3:4], axis=-1, keepdims=True)) + LAMBDA_INIT)

    def reset():
        m_ref[...] = jnp.full(m_ref.shape, -jnp.inf, F32)
        acc_ref[...] = jnp.zeros(acc_ref.shape, F32)

    def scores(slot, task):
        i, t = task
        qt = qt_ref[0, i]
        zero = jnp.zeros_like(qt)
        k = k_ref[0, pl.ds(pl.multiple_of(t * tq, tq), tq), :]
        for h in heads:
            st = _dot(k, jnp.where(first_head, qt, zero) if h == 0 else jnp.where(first_head, zero, qt))
            s_refs[slot][h] = st
            mx_refs[slot][h] = jnp.max(st, axis=0, keepdims=True)

    def attend(slot, task, diagonal):
        i, t = task
        vt = jnp.concatenate([vt_ref[0, t], ones], axis=0)
        for h in heads:
            st = s_refs[slot][h]
            mx = mx_refs[slot][h]
            if diagonal:
                key = lax.broadcasted_iota(jnp.int32, (tq, tq), 0)
                qry = lax.broadcasted_iota(jnp.int32, (tq, tq), 1)
                st = jnp.where(key <= qry, st, -jnp.inf)
                mx = jnp.max(st, axis=0, keepdims=True)
            m_new = jnp.maximum(m_ref[h], mx)
            a = jnp.exp2(m_ref[h] - m_new)
            p = jnp.exp2(st - m_new)
            m_ref[h] = m_new
            acc_ref[h] = a * acc_ref[h] + _dot(vt, p.astype(BF16))
        if diagonal:
            num = [acc_ref[h, 0:LANES, :] for h in heads]
            den = [acc_ref[h, LANES:LANES + 1, :] for h in heads]
            o = (num[0] / den[0] - lam * (num[1] / den[1])).T
            o = o * lax.rsqrt(jnp.mean(o * o, axis=-1, keepdims=True) + NORM_EPS)
            o = (o * nw_ref[...] * (1.0 - LAMBDA_INIT)).astype(BF16)
            o_ref[0, pl.ds(pl.multiple_of(i * tq, tq), tq), :] = o
            reset()

    def following(task):
        i, t = task
        closes = t == i
        return jnp.where(closes, jnp.minimum(i + 1, nblk - 1), i), jnp.where(closes, 0, t + 1)

    def two_tasks(_, task0):
        task1 = following(task0)
        task2 = following(task1)
        diag0 = task0[0] == task0[1]
        diag1 = task1[0] == task1[1]

        def block(d0, d1):
            scores(1, task1)
            attend(0, task0, d0)
            scores(0, task2)
            attend(1, task1, d1)

        pl.when(jnp.logical_not(jnp.logical_or(diag0, diag1)))(lambda: block(False, False))
        pl.when(diag1)(lambda: block(False, True))
        pl.when(diag0)(lambda: block(True, False))
        return task2

    n_tasks = nblk * (nblk + 1) // 2
    assert nblk >= 2 and n_tasks % 2 == 0
    reset()
    first = (jnp.int32(0), jnp.int32(0))
    scores(0, first)
    lax.fori_loop(0, n_tasks // 2, two_tasks, first)


def _diff_attn(dqt, dk, dvt, lam_vecs, nw):
    b, nblk, _, tq = dqt.shape
    s = nblk * tq
    return pl.pallas_call(
        _diff_attn_body,
        grid=(b, DIFF_HEADS),
        in_specs=[
            pl.BlockSpec((1, nblk, LANES, tq), lambda bi, j: (bi, 0, j, 0)),
            pl.BlockSpec((1, s, LANES), lambda bi, j: (bi, 0, j)),
            pl.BlockSpec((1, nblk, LANES, tq), lambda bi, j: (bi, 0, j, 0)),
            _resident((4, HEAD_DIM)),
            pl.BlockSpec((1, LANES), lambda bi, j: (0, j)),
        ],
        out_specs=pl.BlockSpec((1, s, LANES), lambda bi, j: (bi, 0, j)),
        out_shape=jax.ShapeDtypeStruct((b, s, DIFF_WIDTH), BF16),
        scratch_shapes=[pltpu.VMEM((2, tq, tq), F32)] * 2
                       + [pltpu.VMEM((2, 1, tq), F32)] * 2
                       + [pltpu.VMEM((2, 1, tq), F32),
                          pltpu.VMEM((2, LANES + ONES_ROWS, tq), F32)],
        compiler_params=pltpu.CompilerParams(
            dimension_semantics=("arbitrary", "arbitrary"), vmem_limit_bytes=VMEM_LIMIT),
        name="diff_attn",
    )(dqt, dk, dvt, lam_vecs, nw)


def _rotation_inv_freqs():
    half = HEAD_DIM // 2
    inv_ret = 1.0 / (RET_THETA ** jnp.linspace(0.0, 1.0, half, dtype=F32))
    inv_rope = 1.0 / (ROPE_THETA ** (jnp.arange(0, HEAD_DIM, 2, dtype=F32) / HEAD_DIM))
    reps = LANES // half
    return jnp.stack([jnp.tile(inv_ret, reps), jnp.tile(inv_rope, reps)])


def _retention_tables():
    h = RET_HEADS
    log_g = jnp.log(1.0 - 2.0 ** (-5.0 - jnp.arange(h, dtype=F32)))
    idx = jnp.arange(CHUNK, dtype=F32)
    rel = idx[:, None] - idx[None, :]
    intra = jnp.where(rel >= 0, jnp.exp(log_g[:, None, None] * jnp.maximum(rel, 0.0)), 0.0)
    q_decay = jnp.exp(log_g[:, None] * (idx + 1.0))
    k_decay = jnp.exp(log_g[:, None] * (CHUNK - 1.0 - idx))
    chunk_decay = jnp.exp(log_g * CHUNK)

    def slab(t):
        return jnp.repeat(t.reshape(h // 2, 2, CHUNK).transpose(0, 2, 1), HEAD_DIM, axis=2)

    cd = jnp.repeat(chunk_decay.reshape(h // 2, 2), HEAD_DIM, axis=1)
    cd = jnp.broadcast_to(cd[:, :, None], (h // 2, LANES, LANES))
    head_of = np.arange(LANES) // HEAD_DIM
    avg = jnp.asarray((head_of[:, None] == head_of[None, :]) / HEAD_DIM, BF16)
    return intra, slab(q_decay), slab(k_decay), cd, jnp.concatenate([avg, avg], axis=0)


def _even_odd_head_perm():
    within = np.concatenate([np.arange(0, HEAD_DIM, 2), np.arange(1, HEAD_DIM, 2)])
    return (np.arange(RET_HEADS)[:, None] * HEAD_DIM + within[None, :]).reshape(-1)


def kernel(x, positions, ffn1_w_gate, ffn1_w_up, ffn1_w_down, ln1_w, ln1_b, w_in, ret_norm_w,
           diff_lambda_q1, diff_lambda_k1, diff_lambda_q2, diff_lambda_k2, diff_norm_w, w_out,
           ln2_w, ln2_b, ffn2_w_gate, ffn2_w_up, ffn2_w_down, ln3_w, ln3_b):
    b, s, d = x.shape
    n = b * s
    l = 0
    xf = x.reshape(n, d)
    pos = positions.reshape(b, s, 1)

    perm = _even_odd_head_perm()
    w = w_in[l]
    w_perm = jnp.concatenate(
        [w[:, 0:RET_WIDTH][:, perm], w[:, RET_WIDTH:2 * RET_WIDTH][:, perm], w[:, 2 * RET_WIDTH:]], axis=1
    ).astype(BF16)
    inv = _rotation_inv_freqs()
    intra, qd, kd, cd, avg = _retention_tables()
    lam_vecs = jnp.stack([diff_lambda_q1[l], diff_lambda_k1[l], diff_lambda_q2[l], diff_lambda_k2[l]])

    x1 = _ffn_ln(xf, ffn1_w_gate[l].astype(BF16), ffn1_w_up[l].astype(BF16), ffn1_w_down[l].astype(BF16),
                 ln1_w[l][None], ln1_b[l][None])
    rq, rk, rv, rg, dqt, dk, dvt = _in_proj(x1.reshape(b, s, d), pos, w_perm, inv)
    ret = _retention(rq, rk, rv, rg, ret_norm_w[l][None], intra, qd, kd, cd, avg)
    dif = _diff_attn(dqt, dk, dvt, lam_vecs, diff_norm_w[l][None])
    mixer = (ret.reshape(n, RET_WIDTH), dif.reshape(n, DIFF_WIDTH), w_out[l].astype(BF16),
             ln2_w[l][None], ln2_b[l][None])
    out = _ffn_ln(x1, ffn2_w_gate[l].astype(BF16), ffn2_w_up[l].astype(BF16), ffn2_w_down[l].astype(BF16),
                  ln3_w[l][None], ln3_b[l][None], mixer=mixer)
    return out.reshape(b, s, d)
```

```python
import functools
import math

import numpy as np
import jax
import jax.numpy as jnp
from jax import lax
from jax.experimental import pallas as pl
from jax.experimental.pallas import tpu as pltpu

F32 = jnp.float32
BF16 = jnp.bfloat16

D_MODEL = 1024
D_FF = 2816
HEAD_DIM = 64
RET_HEADS = 8
RET_WIDTH = RET_HEADS * HEAD_DIM
DIFF_HEADS = 4
DIFF_VDIM = 2 * HEAD_DIM
DIFF_WIDTH = DIFF_HEADS * DIFF_VDIM
GROUP_WIDTH = 512
N_GROUPS = 7
CHUNK = 128
ROPE_THETA = 10000.0
RET_THETA = 10000.0
LN_EPS = 1e-5
NORM_EPS = 1e-6
DEPTH = 1
DEEPNORM_ALPHA = (2.0 * DEPTH) ** 0.25
LAMBDA_INIT = 0.8 - 0.6 * math.exp(-0.3 * 0)
LOG2_E = math.log2(math.e)

LANES = 128
VMEM_LIMIT = 56 * 1024 * 1024

TM_FFN = 512
MXU_TILE = 256
FF_SPLIT = (D_FF // MXU_TILE // 2) * MXU_TILE
TM_PROJ = 512
TS_RET = 1024
ONES_ROWS = 16


def _dot(a, b):
    return jnp.dot(a, b, preferred_element_type=F32)


def _dot_nt(a, b):
    return lax.dot_general(a, b, (((1,), (1,)), ((), ())), preferred_element_type=F32)


def _dot_tn(a, b):
    return lax.dot_general(a, b, (((0,), (0,)), ((), ())), preferred_element_type=F32)


def _layer_norm(z, w, b):
    mu = jnp.mean(z, axis=-1, keepdims=True)
    d = z - mu
    var = jnp.mean(d * d, axis=-1, keepdims=True)
    return d * lax.rsqrt(var + LN_EPS) * w + b


def _resident(shape):
    nd = len(shape)
    return pl.BlockSpec(shape, lambda *_: (0,) * nd, pipeline_mode=pl.Buffered(1))


def _ffn_ln_body(*refs, with_mixer):
    if with_mixer:
        x_ref, ret_ref, dif_ref, wo_ref, mlnw_ref, mlnb_ref, wg_ref, wu_ref, wd_ref, lnw_ref, lnb_ref, o_ref = refs
    else:
        x_ref, wg_ref, wu_ref, wd_ref, lnw_ref, lnb_ref, o_ref = refs
    half = x_ref.shape[0] // 2
    for rows in (slice(0, half), slice(half, 2 * half)):
        x = x_ref[rows, :]
        if with_mixer:
            mix = _dot(ret_ref[rows, :], wo_ref[0:RET_WIDTH, :]) + _dot(dif_ref[rows, :], wo_ref[RET_WIDTH:, :])
            x = _layer_norm(DEEPNORM_ALPHA * x + mix, mlnw_ref[...], mlnb_ref[...])
        xb = x.astype(BF16)
        y = jnp.zeros(x.shape, F32)
        for sl in (slice(0, FF_SPLIT), slice(FF_SPLIT, D_FF)):
            g = _dot(xb, wg_ref[:, sl])
            u = _dot(xb, wu_ref[:, sl])
            h = (g * jax.nn.sigmoid(g) * u).astype(BF16)
            y = y + _dot(h, wd_ref[sl, :])
        z = DEEPNORM_ALPHA * x + 0.5 * y
        o_ref[rows, :] = _layer_norm(z, lnw_ref[...], lnb_ref[...])


def _ffn_ln(x, wg, wu, wd, lnw, lnb, mixer=None):
    n = x.shape[0]
    tm = TM_FFN
    rows = lambda width: pl.BlockSpec((tm, width), lambda i: (i, 0))
    vec = _resident((1, D_MODEL))
    operands, specs = [x], [rows(D_MODEL)]
    if mixer is not None:
        operands += list(mixer)
        specs += [rows(RET_WIDTH), rows(DIFF_WIDTH), _resident((RET_WIDTH + DIFF_WIDTH, D_MODEL)), vec, vec]
    operands += [wg, wu, wd, lnw, lnb]
    specs += [_resident((D_MODEL, D_FF)), _resident((D_MODEL, D_FF)), _resident((D_FF, D_MODEL)), vec, vec]
    return pl.pallas_call(
        functools.partial(_ffn_ln_body, with_mixer=mixer is not None),
        grid=(n // tm,),
        in_specs=specs,
        out_specs=rows(D_MODEL),
        out_shape=jax.ShapeDtypeStruct((n, D_MODEL), F32),
        compiler_params=pltpu.CompilerParams(
            dimension_semantics=("arbitrary",), vmem_limit_bytes=VMEM_LIMIT),
        name="mix_ffn_ln" if mixer is not None else "ffn_ln",
    )(*operands)


def _in_proj_body(x_ref, pos_ref, w_ref, inv_ref,
                  rq_ref, rk_ref, rv_ref, rg_ref, dqt_ref, dk_ref, dvt_ref):
    xb = x_ref[0].astype(BF16)
    pos = pos_ref[0].astype(F32)
    slabs = range(GROUP_WIDTH // LANES)

    def tables(inv):
        ang = inv * pos
        c, s = jnp.cos(ang), jnp.sin(ang)
        return jnp.concatenate([c, c, c, c], axis=0).T, jnp.concatenate([-s, -s, s, s], axis=0).T

    def rotate(h, cs, scale):
        c, s = cs
        outs = []
        for j in slabs:
            hj = h[:, j * LANES:(j + 1) * LANES]
            swapped = pltpu.roll(hj, LANES // 2, 1)
            r = hj * c + swapped * s
            outs.append(r if scale == 1.0 else r * scale)
        return outs

    def group(g):
        return _dot(xb, w_ref[:, g * GROUP_WIDTH:(g + 1) * GROUP_WIDTH])

    def store(ref, outs):
        ref[0] = jnp.concatenate([o.astype(BF16) for o in outs], axis=1)

    def store_transposed(ref, outs):
        for j, o in enumerate(outs):
            ref[0, 0, j * LANES:(j + 1) * LANES, :] = o.T.astype(BF16)

    h0 = group(0)
    ret_cs = tables(inv_ref[0])
    h1 = group(1)
    store(rq_ref, rotate(h0, ret_cs, 1.0))
    h2 = group(2)
    store(rk_ref, rotate(h1, ret_cs, HEAD_DIM ** -0.5))
    h3 = group(3)
    rv_ref[0] = h2.astype(BF16)
    h4 = group(4)
    rg_ref[0] = h3.astype(BF16)
    rope_cs = tables(inv_ref[1])
    h5 = group(5)
    store_transposed(dqt_ref, rotate(h4, rope_cs, HEAD_DIM ** -0.5 * LOG2_E))
    h6 = group(6)
    store(dk_ref, rotate(h5, rope_cs, 1.0))
    store_transposed(dvt_ref, [h6[:, j * LANES:(j + 1) * LANES] for j in slabs])


def _in_proj(x1, pos, w_in, inv):
    b, s, _ = x1.shape
    tm = TM_PROJ
    row_spec = pl.BlockSpec((1, tm, GROUP_WIDTH), lambda bi, i: (bi, i, 0))
    col_spec = pl.BlockSpec((1, 1, GROUP_WIDTH, tm), lambda bi, i: (bi, i, 0, 0))
    row_shape = jax.ShapeDtypeStruct((b, s, GROUP_WIDTH), BF16)
    col_shape = jax.ShapeDtypeStruct((b, s // tm, GROUP_WIDTH, tm), BF16)
    return pl.pallas_call(
        _in_proj_body,
        grid=(b, s // tm),
        in_specs=[
            pl.BlockSpec((1, tm, D_MODEL), lambda bi, i: (bi, i, 0)),
            pl.BlockSpec((1, 1, tm), lambda bi, i: (bi, 0, i)),
            _resident((D_MODEL, N_GROUPS * GROUP_WIDTH)),
            _resident((2, HEAD_DIM // 2, 1)),
        ],
        out_specs=[row_spec, row_spec, row_spec, row_spec, col_spec, row_spec, col_spec],
        out_shape=[row_shape, row_shape, row_shape, row_shape, col_shape, row_shape, col_shape],
        compiler_params=pltpu.CompilerParams(
            dimension_semantics=("arbitrary", "arbitrary"), vmem_limit_bytes=VMEM_LIMIT),
        name="in_proj",
    )(x1, pos, w_in, inv)


def _retention_body(q_ref, k_ref, v_ref, g_ref, nw_ref, intra_ref, qd_ref, kd_ref, cd_ref, avg_ref,
                    o_ref, state_ref, raw_ref):
    @pl.when(pl.program_id(1) == 0)
    def _():
        state_ref[...] = jnp.zeros(state_ref.shape, F32)

    lane = lax.broadcasted_iota(jnp.int32, (CHUNK, LANES), 1)
    row = lax.broadcasted_iota(jnp.int32, (CHUNK, LANES), 0)
    out_a = lane < HEAD_DIM
    head_a = (lane % HEAD_DIM) < (HEAD_DIM // 2)
    same_head = ((row % HEAD_DIM) < (HEAD_DIM // 2)) == out_a
    avg = avg_ref[...]
    pairs = [slice(p * LANES, (p + 1) * LANES) for p in range(RET_HEADS // 2)]

    states = [state_ref[p] for p in range(len(pairs))]
    for c in range(q_ref.shape[1] // CHUNK):
        rows = slice(c * CHUNK, (c + 1) * CHUNK)
        qs = [q_ref[0, rows, cols] for cols in pairs]
        ks = [k_ref[0, rows, cols] for cols in pairs]
        vs = [v_ref[0, rows, cols] for cols in pairs]
        zero = jnp.zeros_like(qs[0])
        sa = [_dot_nt(jnp.where(head_a, q, zero), k) for q, k in zip(qs, ks)]
        sb = [_dot_nt(jnp.where(head_a, zero, q), k) for q, k in zip(qs, ks)]
        cross = [_dot(q, s.astype(BF16)) for q, s in zip(qs, states)]
        kv = [_dot_tn((k.astype(F32) * kd_ref[p]).astype(BF16), v) for p, (k, v) in enumerate(zip(ks, vs))]
        pa = [(s * intra_ref[2 * p]).astype(BF16) for p, s in enumerate(sa)]
        pb = [(s * intra_ref[2 * p + 1]).astype(BF16) for p, s in enumerate(sb)]
        ia = [_dot(x, v) for x, v in zip(pa, vs)]
        ib = [_dot(x, v) for x, v in zip(pb, vs)]
        for p, cols in enumerate(pairs):
            states[p] = states[p] * cd_ref[p] + jnp.where(same_head, kv[p], 0.0)
            raw_ref[rows, cols] = jnp.where(out_a, ia[p], ib[p]) + cross[p] * qd_ref[p]
    for p in range(len(pairs)):
        state_ref[p] = states[p]

    def head_mean(t):
        hi = t.astype(BF16)
        lo = (t - hi.astype(F32)).astype(BF16)
        return _dot(jnp.concatenate([hi, lo], axis=1), avg)

    for cols in pairs:
        o = raw_ref[:, cols]
        d = o - head_mean(o)
        y = d * lax.rsqrt(head_mean(d * d) + NORM_EPS)
        gate = g_ref[0, :, cols].astype(F32)
        o_ref[0, :, cols] = (gate * jax.nn.sigmoid(gate) * (y * nw_ref[:, cols])).astype(BF16)


def _retention(rq, rk, rv, rg, nw, intra, qd, kd, cd, avg):
    b, s, _ = rq.shape
    ts = TS_RET
    seq_spec = pl.BlockSpec((1, ts, RET_WIDTH), lambda bi, si: (bi, si, 0))
    return pl.pallas_call(
        _retention_body,
        grid=(b, s // ts),
        in_specs=[seq_spec, seq_spec, seq_spec, seq_spec,
                  _resident((1, RET_WIDTH)),
                  _resident(intra.shape), _resident(qd.shape), _resident(kd.shape),
                  _resident(cd.shape), _resident(avg.shape)],
        out_specs=seq_spec,
        out_shape=jax.ShapeDtypeStruct((b, s, RET_WIDTH), BF16),
        scratch_shapes=[pltpu.VMEM((RET_HEADS // 2, LANES, LANES), F32),
                        pltpu.VMEM((ts, RET_WIDTH), F32)],
        compiler_params=pltpu.CompilerParams(
            dimension_semantics=("arbitrary", "arbitrary"), vmem_limit_bytes=VMEM_LIMIT),
        name="retention",
    )(rq, rk, rv, rg, nw, intra, qd, kd, cd, avg)


def _diff_attn_body(qt_ref, k_ref, vt_ref, lam_ref, nw_ref, o_ref, s0_ref, s1_ref, mx0_ref, mx1_ref, m_ref, acc_ref):
    nblk, _, tq = qt_ref.shape[1:]
    s_refs, mx_refs = (s0_ref, s1_ref), (mx0_ref, mx1_ref)
    row = lax.broadcasted_iota(jnp.int32, (LANES, 1), 0)
    first_head = (row % HEAD_DIM) < (HEAD_DIM // 2)
    heads = range(2)
    ones = jnp.ones((ONES_ROWS, tq), BF16)
    lam_v = lam_ref[...]
    lam = (jnp.exp(jnp.sum(lam_v[0:1] * lam_v[1:2], axis=-1, keepdims=True))
           - jnp.exp(jnp.sum(lam_v[2:3] * lam_v[3:4], axis=-1, keepdims=True)) + LAMBDA_INIT)

    def reset():
        m_ref[...] = jnp.full(m_ref.shape, -jnp.inf, F32)
        acc_ref[...] = jnp.zeros(acc_ref.shape, F32)

    def scores(slot, task):
        i, t = task
        qt = qt_ref[0, i]
        zero = jnp.zeros_like(qt)
        k = k_ref[0, pl.ds(pl.multiple_of(t * tq, tq), tq), :]
        for h in heads:
            st = _dot(k, jnp.where(first_head, qt, zero) if h == 0 else jnp.where(first_head, zero, qt))
            s_refs[slot][h] = st
            mx_refs[slot][h] = jnp.max(st, axis=0, keepdims=True)

    def attend(slot, task, diagonal):
        i, t = task
        vt = jnp.concatenate([vt_ref[0, t], ones], axis=0)
        for h in heads:
            st = s_refs[slot][h]
            mx = mx_refs[slot][h]
            if diagonal:
                key = lax.broadcasted_iota(jnp.int32, (tq, tq), 0)
                qry = lax.broadcasted_iota(jnp.int32, (tq, tq), 1)
                st = jnp.where(key <= qry, st, -jnp.inf)
                mx = jnp.max(st, axis=0, keepdims=True)
            m_new = jnp.maximum(m_ref[h], mx)
            a = jnp.exp2(m_ref[h] - m_new)
            p = jnp.exp2(st - m_new)
            m_ref[h] = m_new
            acc_ref[h] = a * acc_ref[h] + _dot(vt, p.astype(BF16))
        if diagonal:
            num = [acc_ref[h, 0:LANES, :] for h in heads]
            den = [acc_ref[h, LANES:LANES + 1, :] for h in heads]
            o = (num[0] / den[0] - lam * (num[1] / den[1])).T
            o = o * lax.rsqrt(jnp.mean(o * o, axis=-1, keepdims=True) + NORM_EPS)
            o = (o * nw_ref[...] * (1.0 - LAMBDA_INIT)).astype(BF16)
            o_ref[0, pl.ds(pl.multiple_of(i * tq, tq), tq), :] = o
            reset()

    def following(task):
        i, t = task
        closes = t == i
        return jnp.where(closes, jnp.minimum(i + 1, nblk - 1), i), jnp.where(closes, 0, t + 1)

    def two_tasks(_, task0):
        task1 = following(task0)
        task2 = following(task1)
        diag0 = task0[0] == task0[1]
        diag1 = task1[0] == task1[1]

        def block(d0, d1):
            scores(1, task1)
            attend(0, task0, d0)
            scores(0, task2)
            attend(1, task1, d1)

        pl.when(jnp.logical_not(jnp.logical_or(diag0, diag1)))(lambda: block(False, False))
        pl.when(diag1)(lambda: block(False, True))
        pl.when(diag0)(lambda: block(True, False))
        return task2

    n_tasks = nblk * (nblk + 1) // 2
    assert nblk >= 2 and n_tasks % 2 == 0
    reset()
    first = (jnp.int32(0), jnp.int32(0))
    scores(0, first)
    lax.fori_loop(0, n_tasks // 2, two_tasks, first)


def _diff_attn(dqt, dk, dvt, lam_vecs, nw):
    b, nblk, _, tq = dqt.shape
    s = nblk * tq
    return pl.pallas_call(
        _diff_attn_body,
        grid=(b, DIFF_HEADS),
        in_specs=[
            pl.BlockSpec((1, nblk, LANES, tq), lambda bi, j: (bi, 0, j, 0)),
            pl.BlockSpec((1, s, LANES), lambda bi, j: (bi, 0, j)),
            pl.BlockSpec((1, nblk, LANES, tq), lambda bi, j: (bi, 0, j, 0)),
            _resident((4, HEAD_DIM)),
            pl.BlockSpec((1, LANES), lambda bi, j: (0, j)),
        ],
        out_specs=pl.BlockSpec((1, s, LANES), lambda bi, j: (bi, 0, j)),
        out_shape=jax.ShapeDtypeStruct((b, s, DIFF_WIDTH), BF16),
        scratch_shapes=[pltpu.VMEM((2, tq, tq), F32)] * 2
                       + [pltpu.VMEM((2, 1, tq), F32)] * 2
                       + [pltpu.VMEM((2, 1, tq), F32),
                          pltpu.VMEM((2, LANES + ONES_ROWS, tq), F32)],
        compiler_params=pltpu.CompilerParams(
            dimension_semantics=("arbitrary", "arbitrary"), vmem_limit_bytes=VMEM_LIMIT),
        name="diff_attn",
    )(dqt, dk, dvt, lam_vecs, nw)


def _rotation_inv_freqs():
    half = HEAD_DIM // 2
    inv_ret = 1.0 / (RET_THETA ** jnp.linspace(0.0, 1.0, half, dtype=F32))
    inv_rope = 1.0 / (ROPE_THETA ** (jnp.arange(0, HEAD_DIM, 2, dtype=F32) / HEAD_DIM))
    return jnp.stack([inv_ret, inv_rope])[:, :, None]


def _retention_tables():
    h = RET_HEADS
    log_g = jnp.log(1.0 - 2.0 ** (-5.0 - jnp.arange(h, dtype=F32)))
    idx = jnp.arange(CHUNK, dtype=F32)
    rel = idx[:, None] - idx[None, :]
    intra = jnp.where(rel >= 0, jnp.exp(log_g[:, None, None] * jnp.maximum(rel, 0.0)), 0.0)
    q_decay = jnp.exp(log_g[:, None] * (idx + 1.0))
    k_decay = jnp.exp(log_g[:, None] * (CHUNK - 1.0 - idx))
    chunk_decay = jnp.exp(log_g * CHUNK)

    value_head = np.arange(LANES) // HEAD_DIM
    qk_head = (np.arange(LANES) % HEAD_DIM) // (HEAD_DIM // 2)

    def slab(t, head_of_lane):
        return t.reshape(h // 2, 2, CHUNK)[:, head_of_lane, :].transpose(0, 2, 1)

    cd = chunk_decay.reshape(h // 2, 2)[:, qk_head]
    cd = jnp.broadcast_to(cd[:, :, None], (h // 2, LANES, LANES))
    avg = jnp.asarray((value_head[:, None] == value_head[None, :]) / HEAD_DIM, BF16)
    return intra, slab(q_decay, value_head), slab(k_decay, qk_head), cd, jnp.concatenate([avg, avg], axis=0)


def _slab_perm(w, even_odd):
    k, half, slabs = w.shape[0], HEAD_DIM // 2, GROUP_WIDTH // LANES
    if even_odd:
        t = w.reshape(k, slabs, 2, half, 2).transpose(0, 1, 4, 2, 3)
    else:
        t = w.reshape(k, slabs, 2, 2, half).transpose(0, 1, 3, 2, 4)
    return t.reshape(k, GROUP_WIDTH)


def kernel(x, positions, ffn1_w_gate, ffn1_w_up, ffn1_w_down, ln1_w, ln1_b, w_in, ret_norm_w,
           diff_lambda_q1, diff_lambda_k1, diff_lambda_q2, diff_lambda_k2, diff_norm_w, w_out,
           ln2_w, ln2_b, ffn2_w_gate, ffn2_w_up, ffn2_w_down, ln3_w, ln3_b):
    b, s, d = x.shape
    n = b * s
    l = 0
    xf = x.reshape(n, d)
    pos = positions.reshape(b, 1, s)

    groups = [w_in[l][:, g * GROUP_WIDTH:(g + 1) * GROUP_WIDTH].astype(BF16) for g in range(N_GROUPS)]
    for g, even_odd in ((0, True), (1, True), (4, False), (5, False)):
        groups[g] = _slab_perm(groups[g], even_odd)
    w_perm = jnp.concatenate(groups, axis=1)
    inv = _rotation_inv_freqs()
    intra, qd, kd, cd, avg = _retention_tables()
    lam_vecs = jnp.stack([diff_lambda_q1[l], diff_lambda_k1[l], diff_lambda_q2[l], diff_lambda_k2[l]])

    x1 = _ffn_ln(xf, ffn1_w_gate[l].astype(BF16), ffn1_w_up[l].astype(BF16), ffn1_w_down[l].astype(BF16),
                 ln1_w[l][None], ln1_b[l][None])
    rq, rk, rv, rg, dqt, dk, dvt = _in_proj(x1.reshape(b, s, d), pos, w_perm, inv)
    ret = _retention(rq, rk, rv, rg, ret_norm_w[l][None], intra, qd, kd, cd, avg)
    dif = _diff_attn(dqt, dk, dvt, lam_vecs, diff_norm_w[l][None])
    mixer = (ret.reshape(n, RET_WIDTH), dif.reshape(n, DIFF_WIDTH), w_out[l].astype(BF16),
             ln2_w[l][None], ln2_b[l][None])
    out = _ffn_ln(x1, ffn2_w_gate[l].astype(BF16), ffn2_w_up[l].astype(BF16), ffn2_w_down[l].astype(BF16),
                  ln3_w[l][None], ln3_b[l][None], mixer=mixer)
    return out.reshape(b, s, d)
```

```python
import functools
import math

import numpy as np
import jax
import jax.numpy as jnp
from jax import lax
from jax.experimental import pallas as pl
from jax.experimental.pallas import tpu as pltpu

F32 = jnp.float32
BF16 = jnp.bfloat16

D_MODEL = 1024
D_FF = 2816
HEAD_DIM = 64
RET_HEADS = 8
RET_WIDTH = RET_HEADS * HEAD_DIM
DIFF_HEADS = 4
DIFF_VDIM = 2 * HEAD_DIM
DIFF_WIDTH = DIFF_HEADS * DIFF_VDIM
GROUP_WIDTH = 512
N_GROUPS = 7
CHUNK = 128
ROPE_THETA = 10000.0
RET_THETA = 10000.0
LN_EPS = 1e-5
NORM_EPS = 1e-6
DEPTH = 1
DEEPNORM_ALPHA = (2.0 * DEPTH) ** 0.25
LAMBDA_INIT = 0.8 - 0.6 * math.exp(-0.3 * 0)
LOG2_E = math.log2(math.e)

LANES = 128
VMEM_LIMIT = 56 * 1024 * 1024

TM_FFN = 1024
FFN_ROW_PARTS = 4
MXU_TILE = 256
FF_SPLIT = (D_FF // MXU_TILE // 2) * MXU_TILE
TM_PROJ = 512
TS_RET = 1024
ONES_ROWS = 16
ATT_TASKS_PER_BLOCK = 12


def _dot(a, b):
    return jnp.dot(a, b, preferred_element_type=F32)


def _dot_nt(a, b):
    return lax.dot_general(a, b, (((1,), (1,)), ((), ())), preferred_element_type=F32)


def _dot_tn(a, b):
    return lax.dot_general(a, b, (((0,), (0,)), ((), ())), preferred_element_type=F32)


def _layer_norm(z, w, b):
    mu = jnp.mean(z, axis=-1, keepdims=True)
    d = z - mu
    var = jnp.mean(d * d, axis=-1, keepdims=True)
    return d * lax.rsqrt(var + LN_EPS) * w + b


def _resident(shape):
    nd = len(shape)
    return pl.BlockSpec(shape, lambda *_: (0,) * nd, pipeline_mode=pl.Buffered(1))


def _ffn_ln_body(*refs, with_mixer):
    if with_mixer:
        x_ref, ret_ref, dif_ref, wo_ref, mlnw_ref, mlnb_ref, wg_ref, wu_ref, wd_ref, lnw_ref, lnb_ref, o_ref = refs
    else:
        x_ref, wg_ref, wu_ref, wd_ref, lnw_ref, lnb_ref, o_ref = refs
    part = x_ref.shape[0] // FFN_ROW_PARTS
    for rows in (slice(r * part, (r + 1) * part) for r in range(FFN_ROW_PARTS)):
        x = x_ref[rows, :]
        if with_mixer:
            mix = _dot(ret_ref[rows, :], wo_ref[0:RET_WIDTH, :]) + _dot(dif_ref[rows, :], wo_ref[RET_WIDTH:, :])
            x = _layer_norm(DEEPNORM_ALPHA * x + mix, mlnw_ref[...], mlnb_ref[...])
        xb = x.astype(BF16)
        y = jnp.zeros(x.shape, F32)
        for sl in (slice(0, FF_SPLIT), slice(FF_SPLIT, D_FF)):
            g = _dot(xb, wg_ref[:, sl])
            u = _dot(xb, wu_ref[:, sl])
            h = (g * jax.nn.sigmoid(g) * u).astype(BF16)
            y = y + _dot(h, wd_ref[sl, :])
        z = DEEPNORM_ALPHA * x + 0.5 * y
        o_ref[rows, :] = _layer_norm(z, lnw_ref[...], lnb_ref[...])


def _ffn_ln(x, wg, wu, wd, lnw, lnb, mixer=None):
    n = x.shape[0]
    tm = TM_FFN
    rows = lambda width: pl.BlockSpec((tm, width), lambda i: (i, 0))
    vec = _resident((1, D_MODEL))
    operands, specs = [x], [rows(D_MODEL)]
    if mixer is not None:
        operands += list(mixer)
        specs += [rows(RET_WIDTH), rows(DIFF_WIDTH), _resident((RET_WIDTH + DIFF_WIDTH, D_MODEL)), vec, vec]
    operands += [wg, wu, wd, lnw, lnb]
    specs += [_resident((D_MODEL, D_FF)), _resident((D_MODEL, D_FF)), _resident((D_FF, D_MODEL)), vec, vec]
    return pl.pallas_call(
        functools.partial(_ffn_ln_body, with_mixer=mixer is not None),
        grid=(n // tm,),
        in_specs=specs,
        out_specs=rows(D_MODEL),
        out_shape=jax.ShapeDtypeStruct((n, D_MODEL), F32),
        compiler_params=pltpu.CompilerParams(
            dimension_semantics=("arbitrary",), vmem_limit_bytes=VMEM_LIMIT),
        name="mix_ffn_ln" if mixer is not None else "ffn_ln",
    )(*operands)


def _in_proj_body(x_ref, pos_ref, w_ref, inv_ref,
                  rq_ref, rk_ref, rv_ref, rg_ref, dqt_ref, dk_ref, dvt_ref):
    xb = x_ref[0].astype(BF16)
    pos = pos_ref[0].astype(F32)
    slabs = range(GROUP_WIDTH // LANES)

    def tables(inv):
        ang = inv * pos
        c, s = jnp.cos(ang), jnp.sin(ang)
        return jnp.concatenate([c, c, c, c], axis=0).T, jnp.concatenate([-s, -s, s, s], axis=0).T

    def rotate(h, cs, scale):
        c, s = cs
        outs = []
        for j in slabs:
            hj = h[:, j * LANES:(j + 1) * LANES]
            swapped = pltpu.roll(hj, LANES // 2, 1)
            r = hj * c + swapped * s
            outs.append(r if scale == 1.0 else r * scale)
        return outs

    def group(g):
        return _dot(xb, w_ref[:, g * GROUP_WIDTH:(g + 1) * GROUP_WIDTH])

    def store(ref, outs):
        ref[0] = jnp.concatenate([o.astype(BF16) for o in outs], axis=1)

    def store_transposed(ref, outs):
        for j, o in enumerate(outs):
            ref[0, 0, j * LANES:(j + 1) * LANES, :] = o.T.astype(BF16)

    h0 = group(0)
    ret_cs = tables(inv_ref[0])
    h1 = group(1)
    store(rq_ref, rotate(h0, ret_cs, 1.0))
    h2 = group(2)
    store(rk_ref, rotate(h1, ret_cs, HEAD_DIM ** -0.5))
    h3 = group(3)
    rv_ref[0] = h2.astype(BF16)
    h4 = group(4)
    rg_ref[0] = h3.astype(BF16)
    rope_cs = tables(inv_ref[1])
    h5 = group(5)
    store_transposed(dqt_ref, rotate(h4, rope_cs, HEAD_DIM ** -0.5 * LOG2_E))
    h6 = group(6)
    store(dk_ref, rotate(h5, rope_cs, 1.0))
    store_transposed(dvt_ref, [h6[:, j * LANES:(j + 1) * LANES] for j in slabs])


def _in_proj(x1, pos, w_in, inv):
    b, s, _ = x1.shape
    tm = TM_PROJ
    row_spec = pl.BlockSpec((1, tm, GROUP_WIDTH), lambda bi, i: (bi, i, 0))
    col_spec = pl.BlockSpec((1, 1, GROUP_WIDTH, tm), lambda bi, i: (bi, i, 0, 0))
    row_shape = jax.ShapeDtypeStruct((b, s, GROUP_WIDTH), BF16)
    col_shape = jax.ShapeDtypeStruct((b, s // tm, GROUP_WIDTH, tm), BF16)
    return pl.pallas_call(
        _in_proj_body,
        grid=(b, s // tm),
        in_specs=[
            pl.BlockSpec((1, tm, D_MODEL), lambda bi, i: (bi, i, 0)),
            pl.BlockSpec((1, 1, tm), lambda bi, i: (bi, 0, i)),
            _resident((D_MODEL, N_GROUPS * GROUP_WIDTH)),
            _resident((2, HEAD_DIM // 2, 1)),
        ],
        out_specs=[row_spec, row_spec, row_spec, row_spec, col_spec, row_spec, col_spec],
        out_shape=[row_shape, row_shape, row_shape, row_shape, col_shape, row_shape, col_shape],
        compiler_params=pltpu.CompilerParams(
            dimension_semantics=("arbitrary", "arbitrary"), vmem_limit_bytes=VMEM_LIMIT),
        name="in_proj",
    )(x1, pos, w_in, inv)


def _retention_body(q_ref, k_ref, v_ref, g_ref, nw_ref, intra_ref, qd_ref, kd_ref, cd_ref, avg_ref,
                    o_ref, state_ref, raw_ref):
    @pl.when(pl.program_id(1) == 0)
    def _():
        state_ref[...] = jnp.zeros(state_ref.shape, F32)

    lane = lax.broadcasted_iota(jnp.int32, (CHUNK, LANES), 1)
    row = lax.broadcasted_iota(jnp.int32, (CHUNK, LANES), 0)
    out_a = lane < HEAD_DIM
    head_a = (lane % HEAD_DIM) < (HEAD_DIM // 2)
    same_head = ((row % HEAD_DIM) < (HEAD_DIM // 2)) == out_a
    avg = avg_ref[...]
    pairs = [slice(p * LANES, (p + 1) * LANES) for p in range(RET_HEADS // 2)]

    states = [state_ref[p] for p in range(len(pairs))]
    for c in range(q_ref.shape[1] // CHUNK):
        rows = slice(c * CHUNK, (c + 1) * CHUNK)
        qs = [q_ref[0, rows, cols] for cols in pairs]
        ks = [k_ref[0, rows, cols] for cols in pairs]
        vs = [v_ref[0, rows, cols] for cols in pairs]
        zero = jnp.zeros_like(qs[0])
        sa = [_dot_nt(jnp.where(head_a, q, zero), k) for q, k in zip(qs, ks)]
        sb = [_dot_nt(jnp.where(head_a, zero, q), k) for q, k in zip(qs, ks)]
        cross = [_dot(q, s.astype(BF16)) for q, s in zip(qs, states)]
        kv = [_dot_tn((k.astype(F32) * kd_ref[p]).astype(BF16), v) for p, (k, v) in enumerate(zip(ks, vs))]
        pa = [(s * intra_ref[2 * p]).astype(BF16) for p, s in enumerate(sa)]
        pb = [(s * intra_ref[2 * p + 1]).astype(BF16) for p, s in enumerate(sb)]
        ia = [_dot(x, v) for x, v in zip(pa, vs)]
        ib = [_dot(x, v) for x, v in zip(pb, vs)]
        for p, cols in enumerate(pairs):
            states[p] = states[p] * cd_ref[p] + jnp.where(same_head, kv[p], 0.0)
            raw_ref[rows, cols] = jnp.where(out_a, ia[p], ib[p]) + cross[p] * qd_ref[p]
    for p in range(len(pairs)):
        state_ref[p] = states[p]

    def head_mean(t):
        hi = t.astype(BF16)
        lo = (t - hi.astype(F32)).astype(BF16)
        return _dot(jnp.concatenate([hi, lo], axis=1), avg)

    for cols in pairs:
        o = raw_ref[:, cols]
        d = o - head_mean(o)
        y = d * lax.rsqrt(head_mean(d * d) + NORM_EPS)
        gate = g_ref[0, :, cols].astype(F32)
        o_ref[0, :, cols] = (gate * jax.nn.sigmoid(gate) * (y * nw_ref[:, cols])).astype(BF16)


def _retention(rq, rk, rv, rg, nw, intra, qd, kd, cd, avg):
    b, s, _ = rq.shape
    ts = TS_RET
    seq_spec = pl.BlockSpec((1, ts, RET_WIDTH), lambda bi, si: (bi, si, 0))
    return pl.pallas_call(
        _retention_body,
        grid=(b, s // ts),
        in_specs=[seq_spec, seq_spec, seq_spec, seq_spec,
                  _resident((1, RET_WIDTH)),
                  _resident(intra.shape), _resident(qd.shape), _resident(kd.shape),
                  _resident(cd.shape), _resident(avg.shape)],
        out_specs=seq_spec,
        out_shape=jax.ShapeDtypeStruct((b, s, RET_WIDTH), BF16),
        scratch_shapes=[pltpu.VMEM((RET_HEADS // 2, LANES, LANES), F32),
                        pltpu.VMEM((ts, RET_WIDTH), F32)],
        compiler_params=pltpu.CompilerParams(
            dimension_semantics=("arbitrary", "arbitrary"), vmem_limit_bytes=VMEM_LIMIT),
        name="retention",
    )(rq, rk, rv, rg, nw, intra, qd, kd, cd, avg)


def _diff_attn_body(qt_ref, k_ref, vt_ref, lam_ref, nw_ref, o_ref, s0_ref, s1_ref, mx0_ref, mx1_ref, m_ref, acc_ref):
    nblk, _, tq = qt_ref.shape[1:]
    s_refs, mx_refs = (s0_ref, s1_ref), (mx0_ref, mx1_ref)
    row = lax.broadcasted_iota(jnp.int32, (LANES, 1), 0)
    first_head = (row % HEAD_DIM) < (HEAD_DIM // 2)
    heads = range(2)
    ones = jnp.ones((ONES_ROWS, tq), BF16)

    def scores(slot, task):
        i, t = task
        qt = qt_ref[0, i]
        zero = jnp.zeros_like(qt)
        k = k_ref[0, pl.ds(pl.multiple_of(t * tq, tq), tq), :]
        for h in heads:
            st = _dot(k, jnp.where(first_head, qt, zero) if h == 0 else jnp.where(first_head, zero, qt))
            s_refs[slot][h] = st
            mx_refs[slot][h] = jnp.max(st, axis=0, keepdims=True)

    def attend(slot, task, diagonal):
        i, t = task
        vt = jnp.concatenate([vt_ref[0, t], ones], axis=0)
        for h in heads:
            st = s_refs[slot][h]
            if diagonal:
                key = lax.broadcasted_iota(jnp.int32, (tq, tq), 0)
                qry = lax.broadcasted_iota(jnp.int32, (tq, tq), 1)
                st = jnp.where(key <= qry, st, -jnp.inf)
                m_new = jnp.max(st, axis=0, keepdims=True)
                p = jnp.exp2(st - m_new)
                acc_ref[i, h] = _dot(vt, p.astype(BF16))
            else:
                m_old = m_ref[i, h]
                m_new = jnp.maximum(m_old, mx_refs[slot][h])
                p = jnp.exp2(st - m_new)
                acc_ref[i, h] = jnp.exp2(m_old - m_new) * acc_ref[i, h] + _dot(vt, p.astype(BF16))
            m_ref[i, h] = m_new

    diagonal_task = lambda i: (jnp.minimum(i, nblk - 1),) * 2
    assert nblk % 2 == 0
    scores(0, diagonal_task(0))

    def two_diagonals(j, carry):
        i = 2 * j
        scores(1, diagonal_task(i + 1))
        attend(0, diagonal_task(i), True)
        scores(0, diagonal_task(i + 2))
        attend(1, diagonal_task(i + 1), True)
        return carry

    lax.fori_loop(0, nblk // 2, two_diagonals, 0)

    def following(task):
        i, t = task
        last = t + 1 == i
        return jnp.where(last, jnp.minimum(i + 1, nblk - 1), i), jnp.where(last, 0, t + 1)

    def plain_block(_, task):
        for u in range(ATT_TASKS_PER_BLOCK):
            nxt = following(task)
            scores(1 - u % 2, nxt)
            attend(u % 2, task, False)
            task = nxt
        return task

    n_plain = nblk * (nblk - 1) // 2
    assert ATT_TASKS_PER_BLOCK % 2 == 0 and n_plain % ATT_TASKS_PER_BLOCK == 0
    first = (jnp.int32(1), jnp.int32(0))
    scores(0, first)
    lax.fori_loop(0, n_plain // ATT_TASKS_PER_BLOCK, plain_block, first)

    lam_v = lam_ref[...]
    lam = (jnp.exp(jnp.sum(lam_v[0:1] * lam_v[1:2], axis=-1, keepdims=True))
           - jnp.exp(jnp.sum(lam_v[2:3] * lam_v[3:4], axis=-1, keepdims=True)) + LAMBDA_INIT)

    def finish_tile(i, carry):
        num = [acc_ref[i, h, 0:LANES, :] for h in heads]
        den = [acc_ref[i, h, LANES:LANES + 1, :] for h in heads]
        o = (num[0] / den[0] - lam * (num[1] / den[1])).T
        o = o * lax.rsqrt(jnp.mean(o * o, axis=-1, keepdims=True) + NORM_EPS)
        o_ref[0, pl.ds(pl.multiple_of(i * tq, tq), tq), :] = (o * nw_ref[...] * (1.0 - LAMBDA_INIT)).astype(BF16)
        return carry

    lax.fori_loop(0, nblk, finish_tile, 0)


def _diff_attn(dqt, dk, dvt, lam_vecs, nw):
    b, nblk, _, tq = dqt.shape
    s = nblk * tq
    return pl.pallas_call(
        _diff_attn_body,
        grid=(b, DIFF_HEADS),
        in_specs=[
            pl.BlockSpec((1, nblk, LANES, tq), lambda bi, j: (bi, 0, j, 0)),
            pl.BlockSpec((1, s, LANES), lambda bi, j: (bi, 0, j)),
            pl.BlockSpec((1, nblk, LANES, tq), lambda bi, j: (bi, 0, j, 0)),
            _resident((4, HEAD_DIM)),
            pl.BlockSpec((1, LANES), lambda bi, j: (0, j)),
        ],
        out_specs=pl.BlockSpec((1, s, LANES), lambda bi, j: (bi, 0, j)),
        out_shape=jax.ShapeDtypeStruct((b, s, DIFF_WIDTH), BF16),
        scratch_shapes=[pltpu.VMEM((2, tq, tq), F32)] * 2
                       + [pltpu.VMEM((2, 1, tq), F32)] * 2
                       + [pltpu.VMEM((nblk, 2, 1, tq), F32),
                          pltpu.VMEM((nblk, 2, LANES + ONES_ROWS, tq), F32)],
        compiler_params=pltpu.CompilerParams(
            dimension_semantics=("arbitrary", "arbitrary"), vmem_limit_bytes=VMEM_LIMIT),
        name="diff_attn",
    )(dqt, dk, dvt, lam_vecs, nw)


def _rotation_inv_freqs():
    half = HEAD_DIM // 2
    inv_ret = 1.0 / (RET_THETA ** jnp.linspace(0.0, 1.0, half, dtype=F32))
    inv_rope = 1.0 / (ROPE_THETA ** (jnp.arange(0, HEAD_DIM, 2, dtype=F32) / HEAD_DIM))
    return jnp.stack([inv_ret, inv_rope])[:, :, None]


def _retention_tables():
    h = RET_HEADS
    log_g = jnp.log(1.0 - 2.0 ** (-5.0 - jnp.arange(h, dtype=F32)))
    idx = jnp.arange(CHUNK, dtype=F32)
    rel = idx[:, None] - idx[None, :]
    intra = jnp.where(rel >= 0, jnp.exp(log_g[:, None, None] * jnp.maximum(rel, 0.0)), 0.0)
    q_decay = jnp.exp(log_g[:, None] * (idx + 1.0))
    k_decay = jnp.exp(log_g[:, None] * (CHUNK - 1.0 - idx))
    chunk_decay = jnp.exp(log_g * CHUNK)

    value_head = np.arange(LANES) // HEAD_DIM
    qk_head = (np.arange(LANES) % HEAD_DIM) // (HEAD_DIM // 2)

    def slab(t, head_of_lane):
        return t.reshape(h // 2, 2, CHUNK)[:, head_of_lane, :].transpose(0, 2, 1)

    cd = chunk_decay.reshape(h // 2, 2)[:, qk_head]
    cd = jnp.broadcast_to(cd[:, :, None], (h // 2, LANES, LANES))
    avg = jnp.asarray((value_head[:, None] == value_head[None, :]) / HEAD_DIM, BF16)
    return intra, slab(q_decay, value_head), slab(k_decay, qk_head), cd, jnp.concatenate([avg, avg], axis=0)


def _slab_perm(w, even_odd):
    k, half, slabs = w.shape[0], HEAD_DIM // 2, GROUP_WIDTH // LANES
    if even_odd:
        t = w.reshape(k, slabs, 2, half, 2).transpose(0, 1, 4, 2, 3)
    else:
        t = w.reshape(k, slabs, 2, 2, half).transpose(0, 1, 3, 2, 4)
    return t.reshape(k, GROUP_WIDTH)


def kernel(x, positions, ffn1_w_gate, ffn1_w_up, ffn1_w_down, ln1_w, ln1_b, w_in, ret_norm_w,
           diff_lambda_q1, diff_lambda_k1, diff_lambda_q2, diff_lambda_k2, diff_norm_w, w_out,
           ln2_w, ln2_b, ffn2_w_gate, ffn2_w_up, ffn2_w_down, ln3_w, ln3_b):
    b, s, d = x.shape
    n = b * s
    l = 0
    xf = x.reshape(n, d)
    pos = positions.reshape(b, 1, s)

    groups = [w_in[l][:, g * GROUP_WIDTH:(g + 1) * GROUP_WIDTH].astype(BF16) for g in range(N_GROUPS)]
    for g, even_odd in ((0, True), (1, True), (4, False), (5, False)):
        groups[g] = _slab_perm(groups[g], even_odd)
    w_perm = jnp.concatenate(groups, axis=1)
    inv = _rotation_inv_freqs()
    intra, qd, kd, cd, avg = _retention_tables()
    lam_vecs = jnp.stack([diff_lambda_q1[l], diff_lambda_k1[l], diff_lambda_q2[l], diff_lambda_k2[l]])

    x1 = _ffn_ln(xf, ffn1_w_gate[l].astype(BF16), ffn1_w_up[l].astype(BF16), ffn1_w_down[l].astype(BF16),
                 ln1_w[l][None], ln1_b[l][None])
    rq, rk, rv, rg, dqt, dk, dvt = _in_proj(x1.reshape(b, s, d), pos, w_perm, inv)
    ret = _retention(rq, rk, rv, rg, ret_norm_w[l][None], intra, qd, kd, cd, avg)
    dif = _diff_attn(dqt, dk, dvt, lam_vecs, diff_norm_w[l][None])
    mixer = (ret.reshape(n, RET_WIDTH), dif.reshape(n, DIFF_WIDTH), w_out[l].astype(BF16),
             ln2_w[l][None], ln2_b[l][None])
    out = _ffn_ln(x1, ffn2_w_gate[l].astype(BF16), ffn2_w_up[l].astype(BF16), ffn2_w_down[l].astype(BF16),
                  ln3_w[l][None], ln3_b[l][None], mixer=mixer)
    return out.reshape(b, s, d)
```

```python
import functools
import math

import numpy as np
import jax
import jax.numpy as jnp
from jax import lax
from jax.experimental import pallas as pl
from jax.experimental.pallas import tpu as pltpu

F32 = jnp.float32
BF16 = jnp.bfloat16

D_MODEL = 1024
D_FF = 2816
HEAD_DIM = 64
RET_HEADS = 8
RET_WIDTH = RET_HEADS * HEAD_DIM
DIFF_HEADS = 4
DIFF_VDIM = 2 * HEAD_DIM
DIFF_WIDTH = DIFF_HEADS * DIFF_VDIM
GROUP_WIDTH = 512
N_GROUPS = 7
CHUNK = 128
ROPE_THETA = 10000.0
RET_THETA = 10000.0
LN_EPS = 1e-5
NORM_EPS = 1e-6
DEPTH = 1
DEEPNORM_ALPHA = (2.0 * DEPTH) ** 0.25
LAMBDA_INIT = 0.8 - 0.6 * math.exp(-0.3 * 0)
LOG2_E = math.log2(math.e)

LANES = 128
VMEM_LIMIT = 56 * 1024 * 1024

TM_FFN = 1024
FFN_ROW_PARTS = 4
MXU_TILE = 256
FF_SPLIT = (D_FF // MXU_TILE // 2) * MXU_TILE
TM_PROJ = 512
TM_IN = 1024
TS_RET = 1024
ONES_ROWS = 16
ATT_TASKS_PER_BLOCK = 12


def _dot(a, b):
    return jnp.dot(a, b, preferred_element_type=F32)


def _dot_nt(a, b):
    return lax.dot_general(a, b, (((1,), (1,)), ((), ())), preferred_element_type=F32)


def _dot_tn(a, b):
    return lax.dot_general(a, b, (((0,), (0,)), ((), ())), preferred_element_type=F32)


def _layer_norm(z, w, b):
    mu = jnp.mean(z, axis=-1, keepdims=True)
    d = z - mu
    var = jnp.mean(d * d, axis=-1, keepdims=True)
    return d * lax.rsqrt(var + LN_EPS) * w + b


def _resident(shape):
    nd = len(shape)
    return pl.BlockSpec(shape, lambda *_: (0,) * nd, pipeline_mode=pl.Buffered(1))


def _ffn_ln_body(*refs, with_mixer):
    if with_mixer:
        x_ref, ret_ref, dif_ref, wo_ref, mlnw_ref, mlnb_ref, wg_ref, wu_ref, wd_ref, lnw_ref, lnb_ref, o_ref = refs
    else:
        x_ref, wg_ref, wu_ref, wd_ref, lnw_ref, lnb_ref, o_ref = refs
    part = x_ref.shape[0] // FFN_ROW_PARTS
    for rows in (slice(r * part, (r + 1) * part) for r in range(FFN_ROW_PARTS)):
        x = x_ref[rows, :]
        if with_mixer:
            mix = _dot(ret_ref[rows, :], wo_ref[0:RET_WIDTH, :]) + _dot(dif_ref[rows, :], wo_ref[RET_WIDTH:, :])
            x = _layer_norm(DEEPNORM_ALPHA * x + mix, mlnw_ref[...], mlnb_ref[...])
        xb = x.astype(BF16)
        y = jnp.zeros(x.shape, F32)
        for sl in (slice(0, FF_SPLIT), slice(FF_SPLIT, D_FF)):
            g = _dot(xb, wg_ref[:, sl])
            u = _dot(xb, wu_ref[:, sl])
            h = (g * jax.nn.sigmoid(g) * u).astype(BF16)
            y = y + _dot(h, wd_ref[sl, :])
        z = DEEPNORM_ALPHA * x + 0.5 * y
        o_ref[rows, :] = _layer_norm(z, lnw_ref[...], lnb_ref[...])


def _ffn_ln(x, wg, wu, wd, lnw, lnb, mixer=None):
    n = x.shape[0]
    tm = TM_FFN
    rows = lambda width: pl.BlockSpec((tm, width), lambda i: (i, 0))
    vec = _resident((1, D_MODEL))
    operands, specs = [x], [rows(D_MODEL)]
    if mixer is not None:
        operands += list(mixer)
        specs += [rows(RET_WIDTH), rows(DIFF_WIDTH), _resident((RET_WIDTH + DIFF_WIDTH, D_MODEL)), vec, vec]
    operands += [wg, wu, wd, lnw, lnb]
    specs += [_resident((D_MODEL, D_FF)), _resident((D_MODEL, D_FF)), _resident((D_FF, D_MODEL)), vec, vec]
    return pl.pallas_call(
        functools.partial(_ffn_ln_body, with_mixer=mixer is not None),
        grid=(n // tm,),
        in_specs=specs,
        out_specs=rows(D_MODEL),
        out_shape=jax.ShapeDtypeStruct((n, D_MODEL), F32),
        compiler_params=pltpu.CompilerParams(
            dimension_semantics=("arbitrary",), vmem_limit_bytes=VMEM_LIMIT),
        name="mix_ffn_ln" if mixer is not None else "ffn_ln",
    )(*operands)


def _in_proj_body(x_ref, pos_ref, w_ref, inv_ref,
                  rq_ref, rk_ref, rv_ref, rg_ref, dqt_ref, dk_ref, dvt_ref):
    slabs = range(GROUP_WIDTH // LANES)
    for part in range(x_ref.shape[1] // TM_PROJ):
        rows = slice(part * TM_PROJ, (part + 1) * TM_PROJ)
        xb = x_ref[0, rows, :].astype(BF16)
        pos = pos_ref[0, :, rows].astype(F32)

        def tables(inv):
            ang = inv * pos
            c, s = jnp.cos(ang), jnp.sin(ang)
            return jnp.concatenate([c, c, c, c], axis=0).T, jnp.concatenate([-s, -s, s, s], axis=0).T

        def rotate(h, cs, scale):
            c, s = cs
            outs = []
            for j in slabs:
                hj = h[:, j * LANES:(j + 1) * LANES]
                r = hj * c + pltpu.roll(hj, LANES // 2, 1) * s
                outs.append(r if scale == 1.0 else r * scale)
            return outs

        def group(g):
            return _dot(xb, w_ref[:, g * GROUP_WIDTH:(g + 1) * GROUP_WIDTH])

        def store(ref, outs):
            ref[0, rows, :] = jnp.concatenate([o.astype(BF16) for o in outs], axis=1)

        def store_transposed(ref, outs):
            for j, o in enumerate(outs):
                ref[0, part, j * LANES:(j + 1) * LANES, :] = o.T.astype(BF16)

        h0 = group(0)
        ret_cs = tables(inv_ref[0])
        h1 = group(1)
        store(rq_ref, rotate(h0, ret_cs, 1.0))
        h2 = group(2)
        store(rk_ref, rotate(h1, ret_cs, HEAD_DIM ** -0.5))
        h3 = group(3)
        rv_ref[0, rows, :] = h2.astype(BF16)
        h4 = group(4)
        rg_ref[0, rows, :] = h3.astype(BF16)
        rope_cs = tables(inv_ref[1])
        h5 = group(5)
        store_transposed(dqt_ref, rotate(h4, rope_cs, HEAD_DIM ** -0.5 * LOG2_E))
        h6 = group(6)
        store(dk_ref, rotate(h5, rope_cs, 1.0))
        store_transposed(dvt_ref, [h6[:, j * LANES:(j + 1) * LANES] for j in slabs])


def _in_proj(x1, pos, w_in, inv):
    b, s, _ = x1.shape
    tm, tq = TM_IN, TM_PROJ
    row_spec = pl.BlockSpec((1, tm, GROUP_WIDTH), lambda bi, i: (bi, i, 0))
    col_spec = pl.BlockSpec((1, tm // tq, GROUP_WIDTH, tq), lambda bi, i: (bi, i, 0, 0))
    row_shape = jax.ShapeDtypeStruct((b, s, GROUP_WIDTH), BF16)
    col_shape = jax.ShapeDtypeStruct((b, s // tq, GROUP_WIDTH, tq), BF16)
    return pl.pallas_call(
        _in_proj_body,
        grid=(b, s // tm),
        in_specs=[
            pl.BlockSpec((1, tm, D_MODEL), lambda bi, i: (bi, i, 0)),
            pl.BlockSpec((1, 1, tm), lambda bi, i: (bi, 0, i)),
            _resident((D_MODEL, N_GROUPS * GROUP_WIDTH)),
            _resident((2, HEAD_DIM // 2, 1)),
        ],
        out_specs=[row_spec, row_spec, row_spec, row_spec, col_spec, row_spec, col_spec],
        out_shape=[row_shape, row_shape, row_shape, row_shape, col_shape, row_shape, col_shape],
        compiler_params=pltpu.CompilerParams(
            dimension_semantics=("arbitrary", "arbitrary"), vmem_limit_bytes=VMEM_LIMIT),
        name="in_proj",
    )(x1, pos, w_in, inv)


def _retention_body(q_ref, k_ref, v_ref, g_ref, nw_ref, intra_ref, qd_ref, kd_ref, cd_ref, avg_ref,
                    o_ref, state_ref, raw_ref):
    @pl.when(pl.program_id(1) == 0)
    def _():
        state_ref[...] = jnp.zeros(state_ref.shape, F32)

    lane = lax.broadcasted_iota(jnp.int32, (CHUNK, LANES), 1)
    row = lax.broadcasted_iota(jnp.int32, (CHUNK, LANES), 0)
    out_a = lane < HEAD_DIM
    head_a = (lane % HEAD_DIM) < (HEAD_DIM // 2)
    same_head = ((row % HEAD_DIM) < (HEAD_DIM // 2)) == out_a
    avg = avg_ref[...]
    pairs = [slice(p * LANES, (p + 1) * LANES) for p in range(RET_HEADS // 2)]

    states = [state_ref[p] for p in range(len(pairs))]
    for c in range(q_ref.shape[1] // CHUNK):
        rows = slice(c * CHUNK, (c + 1) * CHUNK)
        qs = [q_ref[0, rows, cols] for cols in pairs]
        ks = [k_ref[0, rows, cols] for cols in pairs]
        vs = [v_ref[0, rows, cols] for cols in pairs]
        zero = jnp.zeros_like(qs[0])
        sa = [_dot_nt(jnp.where(head_a, q, zero), k) for q, k in zip(qs, ks)]
        sb = [_dot_nt(jnp.where(head_a, zero, q), k) for q, k in zip(qs, ks)]
        cross = [_dot(q, s.astype(BF16)) for q, s in zip(qs, states)]
        kv = [_dot_tn((k.astype(F32) * kd_ref[p]).astype(BF16), v) for p, (k, v) in enumerate(zip(ks, vs))]
        pa = [(s * intra_ref[2 * p]).astype(BF16) for p, s in enumerate(sa)]
        pb = [(s * intra_ref[2 * p + 1]).astype(BF16) for p, s in enumerate(sb)]
        ia = [_dot(x, v) for x, v in zip(pa, vs)]
        ib = [_dot(x, v) for x, v in zip(pb, vs)]
        for p, cols in enumerate(pairs):
            states[p] = states[p] * cd_ref[p] + jnp.where(same_head, kv[p], 0.0)
            raw_ref[rows, cols] = jnp.where(out_a, ia[p], ib[p]) + cross[p] * qd_ref[p]
    for p in range(len(pairs)):
        state_ref[p] = states[p]

    def head_mean(t):
        hi = t.astype(BF16)
        lo = (t - hi.astype(F32)).astype(BF16)
        return _dot(jnp.concatenate([hi, lo], axis=1), avg)

    for cols in pairs:
        o = raw_ref[:, cols]
        d = o - head_mean(o)
        y = d * lax.rsqrt(head_mean(d * d) + NORM_EPS)
        gate = g_ref[0, :, cols].astype(F32)
        o_ref[0, :, cols] = (gate * jax.nn.sigmoid(gate) * (y * nw_ref[:, cols])).astype(BF16)


def _retention(rq, rk, rv, rg, nw, intra, qd, kd, cd, avg):
    b, s, _ = rq.shape
    ts = TS_RET
    seq_spec = pl.BlockSpec((1, ts, RET_WIDTH), lambda bi, si: (bi, si, 0))
    return pl.pallas_call(
        _retention_body,
        grid=(b, s // ts),
        in_specs=[seq_spec, seq_spec, seq_spec, seq_spec,
                  _resident((1, RET_WIDTH)),
                  _resident(intra.shape), _resident(qd.shape), _resident(kd.shape),
                  _resident(cd.shape), _resident(avg.shape)],
        out_specs=seq_spec,
        out_shape=jax.ShapeDtypeStruct((b, s, RET_WIDTH), BF16),
        scratch_shapes=[pltpu.VMEM((RET_HEADS // 2, LANES, LANES), F32),
                        pltpu.VMEM((ts, RET_WIDTH), F32)],
        compiler_params=pltpu.CompilerParams(
            dimension_semantics=("arbitrary", "arbitrary"), vmem_limit_bytes=VMEM_LIMIT),
        name="retention",
    )(rq, rk, rv, rg, nw, intra, qd, kd, cd, avg)


def _diff_attn_body(qt_ref, k_ref, vt_ref, lam_ref, nw_ref, o_ref, s0_ref, s1_ref, mx0_ref, mx1_ref, m_ref, acc_ref):
    nblk, _, tq = qt_ref.shape[1:]
    s_refs, mx_refs = (s0_ref, s1_ref), (mx0_ref, mx1_ref)
    row = lax.broadcasted_iota(jnp.int32, (LANES, 1), 0)
    first_head = (row % HEAD_DIM) < (HEAD_DIM // 2)
    heads = range(2)
    ones = jnp.ones((ONES_ROWS, tq), BF16)

    def scores(slot, task):
        i, t = task
        qt = qt_ref[0, i]
        zero = jnp.zeros_like(qt)
        k = k_ref[0, pl.ds(pl.multiple_of(t * tq, tq), tq), :]
        for h in heads:
            st = _dot(k, jnp.where(first_head, qt, zero) if h == 0 else jnp.where(first_head, zero, qt))
            s_refs[slot][h] = st
            mx_refs[slot][h] = jnp.max(st, axis=0, keepdims=True)

    def attend(slot, task, diagonal):
        i, t = task
        vt = jnp.concatenate([vt_ref[0, t], ones], axis=0)
        for h in heads:
            st = s_refs[slot][h]
            if diagonal:
                key = lax.broadcasted_iota(jnp.int32, (tq, tq), 0)
                qry = lax.broadcasted_iota(jnp.int32, (tq, tq), 1)
                st = jnp.where(key <= qry, st, -jnp.inf)
                m_new = jnp.max(st, axis=0, keepdims=True)
                p = jnp.exp2(st - m_new)
                acc_ref[i, h] = _dot(vt, p.astype(BF16))
            else:
                m_old = m_ref[i, h]
                m_new = jnp.maximum(m_old, mx_refs[slot][h])
                p = jnp.exp2(st - m_new)
                acc_ref[i, h] = jnp.exp2(m_old - m_new) * acc_ref[i, h] + _dot(vt, p.astype(BF16))
            m_ref[i, h] = m_new

    diagonal_task = lambda i: (jnp.minimum(i, nblk - 1),) * 2
    assert nblk % 2 == 0
    scores(0, diagonal_task(0))

    def two_diagonals(j, carry):
        i = 2 * j
        scores(1, diagonal_task(i + 1))
        attend(0, diagonal_task(i), True)
        scores(0, diagonal_task(i + 2))
        attend(1, diagonal_task(i + 1), True)
        return carry

    lax.fori_loop(0, nblk // 2, two_diagonals, 0, unroll=True)

    def following(task):
        i, t = task
        last = t + 1 == i
        return jnp.where(last, jnp.minimum(i + 1, nblk - 1), i), jnp.where(last, 0, t + 1)

    def plain_block(_, task):
        for u in range(ATT_TASKS_PER_BLOCK):
            nxt = following(task)
            scores(1 - u % 2, nxt)
            attend(u % 2, task, False)
            task = nxt
        return task

    n_plain = nblk * (nblk - 1) // 2
    assert ATT_TASKS_PER_BLOCK % 2 == 0 and n_plain % ATT_TASKS_PER_BLOCK == 0
    first = (jnp.int32(1), jnp.int32(0))
    scores(0, first)
    lax.fori_loop(0, n_plain // ATT_TASKS_PER_BLOCK, plain_block, first)

    lam_v = lam_ref[...]
    lam = (jnp.exp(jnp.sum(lam_v[0:1] * lam_v[1:2], axis=-1, keepdims=True))
           - jnp.exp(jnp.sum(lam_v[2:3] * lam_v[3:4], axis=-1, keepdims=True)) + LAMBDA_INIT)

    def finish_tile(i, carry):
        num = [acc_ref[i, h, 0:LANES, :] for h in heads]
        den = [acc_ref[i, h, LANES:LANES + 1, :] for h in heads]
        o = (num[0] / den[0] - lam * (num[1] / den[1])).T
        o = o * lax.rsqrt(jnp.mean(o * o, axis=-1, keepdims=True) + NORM_EPS)
        o_ref[0, pl.ds(pl.multiple_of(i * tq, tq), tq), :] = (o * nw_ref[...] * (1.0 - LAMBDA_INIT)).astype(BF16)
        return carry

    lax.fori_loop(0, nblk, finish_tile, 0, unroll=True)


def _diff_attn(dqt, dk, dvt, lam_vecs, nw):
    b, nblk, _, tq = dqt.shape
    s = nblk * tq
    return pl.pallas_call(
        _diff_attn_body,
        grid=(b, DIFF_HEADS),
        in_specs=[
            pl.BlockSpec((1, nblk, LANES, tq), lambda bi, j: (bi, 0, j, 0)),
            pl.BlockSpec((1, s, LANES), lambda bi, j: (bi, 0, j)),
            pl.BlockSpec((1, nblk, LANES, tq), lambda bi, j: (bi, 0, j, 0)),
            _resident((4, HEAD_DIM)),
            pl.BlockSpec((1, LANES), lambda bi, j: (0, j)),
        ],
        out_specs=pl.BlockSpec((1, s, LANES), lambda bi, j: (bi, 0, j)),
        out_shape=jax.ShapeDtypeStruct((b, s, DIFF_WIDTH), BF16),
        scratch_shapes=[pltpu.VMEM((2, tq, tq), F32)] * 2
                       + [pltpu.VMEM((2, 1, tq), F32)] * 2
                       + [pltpu.VMEM((nblk, 2, 1, tq), F32),
                          pltpu.VMEM((nblk, 2, LANES + ONES_ROWS, tq), F32)],
        compiler_params=pltpu.CompilerParams(
            dimension_semantics=("arbitrary", "arbitrary"), vmem_limit_bytes=VMEM_LIMIT),
        name="diff_attn",
    )(dqt, dk, dvt, lam_vecs, nw)


def _rotation_inv_freqs():
    half = HEAD_DIM // 2
    inv_ret = 1.0 / (RET_THETA ** jnp.linspace(0.0, 1.0, half, dtype=F32))
    inv_rope = 1.0 / (ROPE_THETA ** (jnp.arange(0, HEAD_DIM, 2, dtype=F32) / HEAD_DIM))
    return jnp.stack([inv_ret, inv_rope])[:, :, None]


def _retention_tables():
    h = RET_HEADS
    log_g = jnp.log(1.0 - 2.0 ** (-5.0 - jnp.arange(h, dtype=F32)))
    idx = jnp.arange(CHUNK, dtype=F32)
    rel = idx[:, None] - idx[None, :]
    intra = jnp.where(rel >= 0, jnp.exp(log_g[:, None, None] * jnp.maximum(rel, 0.0)), 0.0)
    q_decay = jnp.exp(log_g[:, None] * (idx + 1.0))
    k_decay = jnp.exp(log_g[:, None] * (CHUNK - 1.0 - idx))
    chunk_decay = jnp.exp(log_g * CHUNK)

    value_head = np.arange(LANES) // HEAD_DIM
    qk_head = (np.arange(LANES) % HEAD_DIM) // (HEAD_DIM // 2)

    def slab(t, head_of_lane):
        return t.reshape(h // 2, 2, CHUNK)[:, head_of_lane, :].transpose(0, 2, 1)

    cd = chunk_decay.reshape(h // 2, 2)[:, qk_head]
    cd = jnp.broadcast_to(cd[:, :, None], (h // 2, LANES, LANES))
    avg = jnp.asarray((value_head[:, None] == value_head[None, :]) / HEAD_DIM, BF16)
    return intra, slab(q_decay, value_head), slab(k_decay, qk_head), cd, jnp.concatenate([avg, avg], axis=0)


def _slab_perm(w, even_odd):
    k, half, slabs = w.shape[0], HEAD_DIM // 2, GROUP_WIDTH // LANES
    if even_odd:
        t = w.reshape(k, slabs, 2, half, 2).transpose(0, 1, 4, 2, 3)
    else:
        t = w.reshape(k, slabs, 2, 2, half).transpose(0, 1, 3, 2, 4)
    return t.reshape(k, GROUP_WIDTH)


def kernel(x, positions, ffn1_w_gate, ffn1_w_up, ffn1_w_down, ln1_w, ln1_b, w_in, ret_norm_w,
           diff_lambda_q1, diff_lambda_k1, diff_lambda_q2, diff_lambda_k2, diff_norm_w, w_out,
           ln2_w, ln2_b, ffn2_w_gate, ffn2_w_up, ffn2_w_down, ln3_w, ln3_b):
    b, s, d = x.shape
    n = b * s
    l = 0
    xf = x.reshape(n, d)
    pos = positions.reshape(b, 1, s)

    groups = [w_in[l][:, g * GROUP_WIDTH:(g + 1) * GROUP_WIDTH].astype(BF16) for g in range(N_GROUPS)]
    for g, even_odd in ((0, True), (1, True), (4, False), (5, False)):
        groups[g] = _slab_perm(groups[g], even_odd)
    w_perm = jnp.concatenate(groups, axis=1)
    inv = _rotation_inv_freqs()
    intra, qd, kd, cd, avg = _retention_tables()
    lam_vecs = jnp.stack([diff_lambda_q1[l], diff_lambda_k1[l], diff_lambda_q2[l], diff_lambda_k2[l]])

    x1 = _ffn_ln(xf, ffn1_w_gate[l].astype(BF16), ffn1_w_up[l].astype(BF16), ffn1_w_down[l].astype(BF16),
                 ln1_w[l][None], ln1_b[l][None])
    rq, rk, rv, rg, dqt, dk, dvt = _in_proj(x1.reshape(b, s, d), pos, w_perm, inv)
    ret = _retention(rq, rk, rv, rg, ret_norm_w[l][None], intra, qd, kd, cd, avg)
    dif = _diff_attn(dqt, dk, dvt, lam_vecs, diff_norm_w[l][None])
    mixer = (ret.reshape(n, RET_WIDTH), dif.reshape(n, DIFF_WIDTH), w_out[l].astype(BF16),
             ln2_w[l][None], ln2_b[l][None])
    out = _ffn_ln(x1, ffn2_w_gate[l].astype(BF16), ffn2_w_up[l].astype(BF16), ffn2_w_down[l].astype(BF16),
                  ln3_w[l][None], ln3_b[l][None], mixer=mixer)
    return out.reshape(b, s, d)
```

```python
import functools
import math

import numpy as np
import jax
import jax.numpy as jnp
from jax import lax
from jax.experimental import pallas as pl
from jax.experimental.pallas import tpu as pltpu

F32 = jnp.float32
BF16 = jnp.bfloat16

D_MODEL = 1024
D_FF = 2816
HEAD_DIM = 64
RET_HEADS = 8
RET_WIDTH = RET_HEADS * HEAD_DIM
DIFF_HEADS = 4
DIFF_VDIM = 2 * HEAD_DIM
DIFF_WIDTH = DIFF_HEADS * DIFF_VDIM
GROUP_WIDTH = 512
N_GROUPS = 7
CHUNK = 128
ROPE_THETA = 10000.0
RET_THETA = 10000.0
LN_EPS = 1e-5
NORM_EPS = 1e-6
DEPTH = 1
DEEPNORM_ALPHA = (2.0 * DEPTH) ** 0.25
LAMBDA_INIT = 0.8 - 0.6 * math.exp(-0.3 * 0)
LOG2_E = math.log2(math.e)

LANES = 128
VMEM_LIMIT = 56 * 1024 * 1024

TM_FFN = 1024
FFN_ROW_PARTS = 4
MXU_TILE = 256
FF_SPLIT = (D_FF // MXU_TILE // 2) * MXU_TILE
TM_PROJ = 512
TM_IN = 1024
TS_RET = 1024
ONES_ROWS = 16
ATT_TASKS_PER_BLOCK = 12


def _dot(a, b):
    return jnp.dot(a, b, preferred_element_type=F32)


def _dot_nt(a, b):
    return lax.dot_general(a, b, (((1,), (1,)), ((), ())), preferred_element_type=F32)


def _dot_tn(a, b):
    return lax.dot_general(a, b, (((0,), (0,)), ((), ())), preferred_element_type=F32)


def _layer_norm(z, w, b):
    mu = jnp.mean(z, axis=-1, keepdims=True)
    d = z - mu
    var = jnp.mean(d * d, axis=-1, keepdims=True)
    return d * lax.rsqrt(var + LN_EPS) * w + b


def _resident(shape):
    nd = len(shape)
    return pl.BlockSpec(shape, lambda *_: (0,) * nd, pipeline_mode=pl.Buffered(1))


def _ffn_ln_body(*refs, with_mixer):
    if with_mixer:
        x_ref, ret_ref, dif_ref, wo_ref, mlnw_ref, mlnb_ref, wg_ref, wu_ref, wd_ref, lnw_ref, lnb_ref, o_ref = refs
    else:
        x_ref, wg_ref, wu_ref, wd_ref, lnw_ref, lnb_ref, o_ref = refs
    part = x_ref.shape[0] // FFN_ROW_PARTS
    parts = [slice(r * part, (r + 1) * part) for r in range(FFN_ROW_PARTS)]

    def ffn_input(rows):
        x = x_ref[rows, :]
        if not with_mixer:
            return x
        mix = _dot(ret_ref[rows, :], wo_ref[0:RET_WIDTH, :]) + _dot(dif_ref[rows, :], wo_ref[RET_WIDTH:, :])
        return _layer_norm(DEEPNORM_ALPHA * x + mix, mlnw_ref[...], mlnb_ref[...])

    x_next = ffn_input(parts[0])
    for r, rows in enumerate(parts):
        x = x_next
        if r + 1 < len(parts):
            x_next = ffn_input(parts[r + 1])
        xb = x.astype(BF16)
        y = jnp.zeros(x.shape, F32)
        for sl in (slice(0, FF_SPLIT), slice(FF_SPLIT, D_FF)):
            g = _dot(xb, wg_ref[:, sl])
            u = _dot(xb, wu_ref[:, sl])
            h = (g * jax.nn.sigmoid(g) * u).astype(BF16)
            y = y + _dot(h, wd_ref[sl, :])
        z = DEEPNORM_ALPHA * x + 0.5 * y
        o_ref[rows, :] = _layer_norm(z, lnw_ref[...], lnb_ref[...])


def _ffn_ln(x, wg, wu, wd, lnw, lnb, mixer=None):
    n = x.shape[0]
    tm = TM_FFN
    rows = lambda width: pl.BlockSpec((tm, width), lambda i: (i, 0))
    vec = _resident((1, D_MODEL))
    operands, specs = [x], [rows(D_MODEL)]
    if mixer is not None:
        operands += list(mixer)
        specs += [rows(RET_WIDTH), rows(DIFF_WIDTH), _resident((RET_WIDTH + DIFF_WIDTH, D_MODEL)), vec, vec]
    operands += [wg, wu, wd, lnw, lnb]
    specs += [_resident((D_MODEL, D_FF)), _resident((D_MODEL, D_FF)), _resident((D_FF, D_MODEL)), vec, vec]
    return pl.pallas_call(
        functools.partial(_ffn_ln_body, with_mixer=mixer is not None),
        grid=(n // tm,),
        in_specs=specs,
        out_specs=rows(D_MODEL),
        out_shape=jax.ShapeDtypeStruct((n, D_MODEL), F32),
        compiler_params=pltpu.CompilerParams(
            dimension_semantics=("arbitrary",), vmem_limit_bytes=VMEM_LIMIT),
        name="mix_ffn_ln" if mixer is not None else "ffn_ln",
    )(*operands)


def _in_proj_body(x_ref, pos_ref, w_ref, inv_ref,
                  rq_ref, rk_ref, rv_ref, rg_ref, dqt_ref, dk_ref, dvt_ref):
    slabs = range(GROUP_WIDTH // LANES)
    for part in range(x_ref.shape[1] // TM_PROJ):
        rows = slice(part * TM_PROJ, (part + 1) * TM_PROJ)
        xb = x_ref[0, rows, :].astype(BF16)
        pos = pos_ref[0, :, rows].astype(F32)

        def tables(inv):
            ang = inv * pos
            c, s = jnp.cos(ang), jnp.sin(ang)
            return jnp.concatenate([c, c, c, c], axis=0).T, jnp.concatenate([-s, -s, s, s], axis=0).T

        def rotate(h, cs, scale):
            c, s = cs
            outs = []
            for j in slabs:
                hj = h[:, j * LANES:(j + 1) * LANES]
                r = hj * c + pltpu.roll(hj, LANES // 2, 1) * s
                outs.append(r if scale == 1.0 else r * scale)
            return outs

        def group(g):
            return _dot(xb, w_ref[:, g * GROUP_WIDTH:(g + 1) * GROUP_WIDTH])

        def store(ref, outs):
            ref[0, rows, :] = jnp.concatenate([o.astype(BF16) for o in outs], axis=1)

        def store_transposed(ref, outs):
            for j, o in enumerate(outs):
                ref[0, part, j * LANES:(j + 1) * LANES, :] = o.T.astype(BF16)

        h0 = group(0)
        ret_cs = tables(inv_ref[0])
        h1 = group(1)
        store(rq_ref, rotate(h0, ret_cs, 1.0))
        h2 = group(2)
        store(rk_ref, rotate(h1, ret_cs, HEAD_DIM ** -0.5))
        h3 = group(3)
        rv_ref[0, rows, :] = h2.astype(BF16)
        h4 = group(4)
        rg_ref[0, rows, :] = h3.astype(BF16)
        rope_cs = tables(inv_ref[1])
        h5 = group(5)
        store_transposed(dqt_ref, rotate(h4, rope_cs, HEAD_DIM ** -0.5 * LOG2_E))
        h6 = group(6)
        store(dk_ref, rotate(h5, rope_cs, 1.0))
        store_transposed(dvt_ref, [h6[:, j * LANES:(j + 1) * LANES] for j in slabs])


def _in_proj(x1, pos, w_in, inv):
    b, s, _ = x1.shape
    tm, tq = TM_IN, TM_PROJ
    row_spec = pl.BlockSpec((1, tm, GROUP_WIDTH), lambda bi, i: (bi, i, 0))
    col_spec = pl.BlockSpec((1, tm // tq, GROUP_WIDTH, tq), lambda bi, i: (bi, i, 0, 0))
    row_shape = jax.ShapeDtypeStruct((b, s, GROUP_WIDTH), BF16)
    col_shape = jax.ShapeDtypeStruct((b, s // tq, GROUP_WIDTH, tq), BF16)
    return pl.pallas_call(
        _in_proj_body,
        grid=(b, s // tm),
        in_specs=[
            pl.BlockSpec((1, tm, D_MODEL), lambda bi, i: (bi, i, 0)),
            pl.BlockSpec((1, 1, tm), lambda bi, i: (bi, 0, i)),
            _resident((D_MODEL, N_GROUPS * GROUP_WIDTH)),
            _resident((2, HEAD_DIM // 2, 1)),
        ],
        out_specs=[row_spec, row_spec, row_spec, row_spec, col_spec, row_spec, col_spec],
        out_shape=[row_shape, row_shape, row_shape, row_shape, col_shape, row_shape, col_shape],
        compiler_params=pltpu.CompilerParams(
            dimension_semantics=("arbitrary", "arbitrary"), vmem_limit_bytes=VMEM_LIMIT),
        name="in_proj",
    )(x1, pos, w_in, inv)


def _retention_body(q_ref, k_ref, v_ref, g_ref, nw_ref, intra_ref, qd_ref, kd_ref, cd_ref, avg_ref,
                    o_ref, state_ref, raw_ref):
    @pl.when(pl.program_id(1) == 0)
    def _():
        state_ref[...] = jnp.zeros(state_ref.shape, F32)

    lane = lax.broadcasted_iota(jnp.int32, (CHUNK, LANES), 1)
    row = lax.broadcasted_iota(jnp.int32, (CHUNK, LANES), 0)
    out_a = lane < HEAD_DIM
    head_a = (lane % HEAD_DIM) < (HEAD_DIM // 2)
    same_head = ((row % HEAD_DIM) < (HEAD_DIM // 2)) == out_a
    avg = avg_ref[...]
    pairs = [slice(p * LANES, (p + 1) * LANES) for p in range(RET_HEADS // 2)]

    states = [state_ref[p] for p in range(len(pairs))]
    for c in range(q_ref.shape[1] // CHUNK):
        rows = slice(c * CHUNK, (c + 1) * CHUNK)
        qs = [q_ref[0, rows, cols] for cols in pairs]
        ks = [k_ref[0, rows, cols] for cols in pairs]
        vs = [v_ref[0, rows, cols] for cols in pairs]
        zero = jnp.zeros_like(qs[0])
        sa = [_dot_nt(jnp.where(head_a, q, zero), k) for q, k in zip(qs, ks)]
        sb = [_dot_nt(jnp.where(head_a, zero, q), k) for q, k in zip(qs, ks)]
        cross = [_dot(q, s.astype(BF16)) for q, s in zip(qs, states)]
        kv = [_dot_tn((k.astype(F32) * kd_ref[p]).astype(BF16), v) for p, (k, v) in enumerate(zip(ks, vs))]
        pa = [(s * intra_ref[2 * p]).astype(BF16) for p, s in enumerate(sa)]
        pb = [(s * intra_ref[2 * p + 1]).astype(BF16) for p, s in enumerate(sb)]
        ia = [_dot(x, v) for x, v in zip(pa, vs)]
        ib = [_dot(x, v) for x, v in zip(pb, vs)]
        for p, cols in enumerate(pairs):
            states[p] = states[p] * cd_ref[p] + jnp.where(same_head, kv[p], 0.0)
            raw_ref[rows, cols] = jnp.where(out_a, ia[p], ib[p]) + cross[p] * qd_ref[p]
    for p in range(len(pairs)):
        state_ref[p] = states[p]

    def head_mean(t):
        hi = t.astype(BF16)
        lo = (t - hi.astype(F32)).astype(BF16)
        return _dot(jnp.concatenate([hi, lo], axis=1), avg)

    for cols in pairs:
        o = raw_ref[:, cols]
        d = o - head_mean(o)
        y = d * lax.rsqrt(head_mean(d * d) + NORM_EPS)
        gate = g_ref[0, :, cols].astype(F32)
        o_ref[0, :, cols] = (gate * jax.nn.sigmoid(gate) * (y * nw_ref[:, cols])).astype(BF16)


def _retention(rq, rk, rv, rg, nw, intra, qd, kd, cd, avg):
    b, s, _ = rq.shape
    ts = TS_RET
    seq_spec = pl.BlockSpec((1, ts, RET_WIDTH), lambda bi, si: (bi, si, 0))
    return pl.pallas_call(
        _retention_body,
        grid=(b, s // ts),
        in_specs=[seq_spec, seq_spec, seq_spec, seq_spec,
                  _resident((1, RET_WIDTH)),
                  _resident(intra.shape), _resident(qd.shape), _resident(kd.shape),
                  _resident(cd.shape), _resident(avg.shape)],
        out_specs=seq_spec,
        out_shape=jax.ShapeDtypeStruct((b, s, RET_WIDTH), BF16),
        scratch_shapes=[pltpu.VMEM((RET_HEADS // 2, LANES, LANES), F32),
                        pltpu.VMEM((ts, RET_WIDTH), F32)],
        compiler_params=pltpu.CompilerParams(
            dimension_semantics=("arbitrary", "arbitrary"), vmem_limit_bytes=VMEM_LIMIT),
        name="retention",
    )(rq, rk, rv, rg, nw, intra, qd, kd, cd, avg)


def _diff_attn_body(qt_ref, k_ref, vt_ref, lam_ref, nw_ref, o_ref, s0_ref, s1_ref, mx0_ref, mx1_ref, m_ref, acc_ref):
    nblk, _, tq = qt_ref.shape[1:]
    s_refs, mx_refs = (s0_ref, s1_ref), (mx0_ref, mx1_ref)
    row = lax.broadcasted_iota(jnp.int32, (LANES, 1), 0)
    first_head = (row % HEAD_DIM) < (HEAD_DIM // 2)
    heads = range(2)
    ones = jnp.ones((ONES_ROWS, tq), BF16)

    def scores(slot, task):
        i, t = task
        qt = qt_ref[0, i]
        zero = jnp.zeros_like(qt)
        k = k_ref[0, pl.ds(pl.multiple_of(t * tq, tq), tq), :]
        for h in heads:
            st = _dot(k, jnp.where(first_head, qt, zero) if h == 0 else jnp.where(first_head, zero, qt))
            s_refs[slot][h] = st
            mx_refs[slot][h] = jnp.max(st, axis=0, keepdims=True)

    def attend(slot, task):
        i, t = task
        vt = jnp.concatenate([vt_ref[0, t], ones], axis=0)
        for h in heads:
            m_old = m_ref[i, h]
            m_new = jnp.maximum(m_old, mx_refs[slot][h])
            p = jnp.exp2(s_refs[slot][h] - m_new)
            acc_ref[i, h] = jnp.exp2(m_old - m_new) * acc_ref[i, h] + _dot(vt, p.astype(BF16))
            m_ref[i, h] = m_new

    half = tq // 2
    early, late = slice(0, half), slice(half, tq)
    causal = (lax.broadcasted_iota(jnp.int32, (half, half), 0)
              <= lax.broadcasted_iota(jnp.int32, (half, half), 1))

    def scores_diagonal(slot, i):
        qt = qt_ref[0, i]
        zero = jnp.zeros_like(qt)
        k_early = k_ref[0, i * tq:i * tq + half, :]
        k_late = k_ref[0, i * tq + half:(i + 1) * tq, :]
        for h in heads:
            qh = jnp.where(first_head, qt, zero) if h == 0 else jnp.where(first_head, zero, qt)
            s_refs[slot][h, early, :] = _dot(k_early, qh)
            s_refs[slot][h, late, late] = _dot(k_late, qh[:, late])

    def attend_diagonal(slot, i):
        vt = jnp.concatenate([vt_ref[0, i], ones], axis=0)
        for h in heads:
            s_ee = jnp.where(causal, s_refs[slot][h, early, early], -jnp.inf)
            s_el = s_refs[slot][h, early, late]
            s_ll = jnp.where(causal, s_refs[slot][h, late, late], -jnp.inf)
            m_e = jnp.max(s_ee, axis=0, keepdims=True)
            m_l = jnp.maximum(jnp.max(s_el, axis=0, keepdims=True), jnp.max(s_ll, axis=0, keepdims=True))
            acc_ref[i, h, :, early] = _dot(vt[:, early], jnp.exp2(s_ee - m_e).astype(BF16))
            acc_ref[i, h, :, late] = (_dot(vt[:, early], jnp.exp2(s_el - m_l).astype(BF16))
                                      + _dot(vt[:, late], jnp.exp2(s_ll - m_l).astype(BF16)))
            m_ref[i, h] = jnp.concatenate([m_e, m_l], axis=1)

    scores_diagonal(0, 0)
    for i in range(nblk):
        if i + 1 < nblk:
            scores_diagonal((i + 1) % 2, i + 1)
        attend_diagonal(i % 2, i)

    def following(task):
        i, t = task
        last = t + 1 == i
        return jnp.where(last, jnp.minimum(i + 1, nblk - 1), i), jnp.where(last, 0, t + 1)

    def plain_block(_, task):
        for u in range(ATT_TASKS_PER_BLOCK):
            nxt = following(task)
            scores(1 - u % 2, nxt)
            attend(u % 2, task)
            task = nxt
        return task

    n_plain = nblk * (nblk - 1) // 2
    assert ATT_TASKS_PER_BLOCK % 2 == 0 and n_plain % ATT_TASKS_PER_BLOCK == 0
    first = (jnp.int32(1), jnp.int32(0))
    scores(0, first)
    lax.fori_loop(0, n_plain // ATT_TASKS_PER_BLOCK, plain_block, first)

    lam_v = lam_ref[...]
    lam = (jnp.exp(jnp.sum(lam_v[0:1] * lam_v[1:2], axis=-1, keepdims=True))
           - jnp.exp(jnp.sum(lam_v[2:3] * lam_v[3:4], axis=-1, keepdims=True)) + LAMBDA_INIT)

    def finish_tile(i, carry):
        num = [acc_ref[i, h, 0:LANES, :] for h in heads]
        den = [acc_ref[i, h, LANES:LANES + 1, :] for h in heads]
        o = (num[0] / den[0] - lam * (num[1] / den[1])).T
        o = o * lax.rsqrt(jnp.mean(o * o, axis=-1, keepdims=True) + NORM_EPS)
        o_ref[0, pl.ds(pl.multiple_of(i * tq, tq), tq), :] = (o * nw_ref[...] * (1.0 - LAMBDA_INIT)).astype(BF16)
        return carry

    lax.fori_loop(0, nblk, finish_tile, 0, unroll=True)


def _diff_attn(dqt, dk, dvt, lam_vecs, nw):
    b, nblk, _, tq = dqt.shape
    s = nblk * tq
    return pl.pallas_call(
        _diff_attn_body,
        grid=(b, DIFF_HEADS),
        in_specs=[
            pl.BlockSpec((1, nblk, LANES, tq), lambda bi, j: (bi, 0, j, 0)),
            pl.BlockSpec((1, s, LANES), lambda bi, j: (bi, 0, j)),
            pl.BlockSpec((1, nblk, LANES, tq), lambda bi, j: (bi, 0, j, 0)),
            _resident((4, HEAD_DIM)),
            pl.BlockSpec((1, LANES), lambda bi, j: (0, j)),
        ],
        out_specs=pl.BlockSpec((1, s, LANES), lambda bi, j: (bi, 0, j)),
        out_shape=jax.ShapeDtypeStruct((b, s, DIFF_WIDTH), BF16),
        scratch_shapes=[pltpu.VMEM((2, tq, tq), F32)] * 2
                       + [pltpu.VMEM((2, 1, tq), F32)] * 2
                       + [pltpu.VMEM((nblk, 2, 1, tq), F32),
                          pltpu.VMEM((nblk, 2, LANES + ONES_ROWS, tq), F32)],
        compiler_params=pltpu.CompilerParams(
            dimension_semantics=("arbitrary", "arbitrary"), vmem_limit_bytes=VMEM_LIMIT),
        name="diff_attn",
    )(dqt, dk, dvt, lam_vecs, nw)


def _rotation_inv_freqs():
    half = HEAD_DIM // 2
    inv_ret = 1.0 / (RET_THETA ** jnp.linspace(0.0, 1.0, half, dtype=F32))
    inv_rope = 1.0 / (ROPE_THETA ** (jnp.arange(0, HEAD_DIM, 2, dtype=F32) / HEAD_DIM))
    return jnp.stack([inv_ret, inv_rope])[:, :, None]


def _retention_tables():
    h = RET_HEADS
    log_g = jnp.log(1.0 - 2.0 ** (-5.0 - jnp.arange(h, dtype=F32)))
    idx = jnp.arange(CHUNK, dtype=F32)
    rel = idx[:, None] - idx[None, :]
    intra = jnp.where(rel >= 0, jnp.exp(log_g[:, None, None] * jnp.maximum(rel, 0.0)), 0.0)
    q_decay = jnp.exp(log_g[:, None] * (idx + 1.0))
    k_decay = jnp.exp(log_g[:, None] * (CHUNK - 1.0 - idx))
    chunk_decay = jnp.exp(log_g * CHUNK)

    value_head = np.arange(LANES) // HEAD_DIM
    qk_head = (np.arange(LANES) % HEAD_DIM) // (HEAD_DIM // 2)

    def slab(t, head_of_lane):
        return t.reshape(h // 2, 2, CHUNK)[:, head_of_lane, :].transpose(0, 2, 1)

    cd = chunk_decay.reshape(h // 2, 2)[:, qk_head]
    cd = jnp.broadcast_to(cd[:, :, None], (h // 2, LANES, LANES))
    avg = jnp.asarray((value_head[:, None] == value_head[None, :]) / HEAD_DIM, BF16)
    return intra, slab(q_decay, value_head), slab(k_decay, qk_head), cd, jnp.concatenate([avg, avg], axis=0)


def _slab_perm(w, even_odd):
    k, half, slabs = w.shape[0], HEAD_DIM // 2, GROUP_WIDTH // LANES
    if even_odd:
        t = w.reshape(k, slabs, 2, half, 2).transpose(0, 1, 4, 2, 3)
    else:
        t = w.reshape(k, slabs, 2, 2, half).transpose(0, 1, 3, 2, 4)
    return t.reshape(k, GROUP_WIDTH)


def kernel(x, positions, ffn1_w_gate, ffn1_w_up, ffn1_w_down, ln1_w, ln1_b, w_in, ret_norm_w,
           diff_lambda_q1, diff_lambda_k1, diff_lambda_q2, diff_lambda_k2, diff_norm_w, w_out,
           ln2_w, ln2_b, ffn2_w_gate, ffn2_w_up, ffn2_w_down, ln3_w, ln3_b):
    b, s, d = x.shape
    n = b * s
    l = 0
    xf = x.reshape(n, d)
    pos = positions.reshape(b, 1, s)

    groups = [w_in[l][:, g * GROUP_WIDTH:(g + 1) * GROUP_WIDTH].astype(BF16) for g in range(N_GROUPS)]
    for g, even_odd in ((0, True), (1, True), (4, False), (5, False)):
        groups[g] = _slab_perm(groups[g], even_odd)
    w_perm = jnp.concatenate(groups, axis=1)
    inv = _rotation_inv_freqs()
    intra, qd, kd, cd, avg = _retention_tables()
    lam_vecs = jnp.stack([diff_lambda_q1[l], diff_lambda_k1[l], diff_lambda_q2[l], diff_lambda_k2[l]])

    x1 = _ffn_ln(xf, ffn1_w_gate[l].astype(BF16), ffn1_w_up[l].astype(BF16), ffn1_w_down[l].astype(BF16),
                 ln1_w[l][None], ln1_b[l][None])
    rq, rk, rv, rg, dqt, dk, dvt = _in_proj(x1.reshape(b, s, d), pos, w_perm, inv)
    ret = _retention(rq, rk, rv, rg, ret_norm_w[l][None], intra, qd, kd, cd, avg)
    dif = _diff_attn(dqt, dk, dvt, lam_vecs, diff_norm_w[l][None])
    mixer = (ret.reshape(n, RET_WIDTH), dif.reshape(n, DIFF_WIDTH), w_out[l].astype(BF16),
             ln2_w[l][None], ln2_b[l][None])
    out = _ffn_ln(x1, ffn2_w_gate[l].astype(BF16), ffn2_w_up[l].astype(BF16), ffn2_w_down[l].astype(BF16),
                  ln3_w[l][None], ln3_b[l][None], mixer=mixer)
    return out.reshape(b, s, d)
```

```python
import functools
import math

import numpy as np
import jax
import jax.numpy as jnp
from jax import lax
from jax.experimental import pallas as pl
from jax.experimental.pallas import tpu as pltpu

F32 = jnp.float32
BF16 = jnp.bfloat16

D_MODEL = 1024
D_FF = 2816
HEAD_DIM = 64
RET_HEADS = 8
RET_WIDTH = RET_HEADS * HEAD_DIM
DIFF_HEADS = 4
DIFF_VDIM = 2 * HEAD_DIM
DIFF_WIDTH = DIFF_HEADS * DIFF_VDIM
GROUP_WIDTH = 512
N_GROUPS = 7
CHUNK = 128
ROPE_THETA = 10000.0
RET_THETA = 10000.0
LN_EPS = 1e-5
NORM_EPS = 1e-6
DEPTH = 1
DEEPNORM_ALPHA = (2.0 * DEPTH) ** 0.25
LAMBDA_INIT = 0.8 - 0.6 * math.exp(-0.3 * 0)
LOG2_E = math.log2(math.e)

LANES = 128
VMEM_LIMIT = 56 * 1024 * 1024

TM_FFN = 1024
FFN_ROW_PARTS = 4
MXU_TILE = 256
FF_SPLIT = (D_FF // MXU_TILE // 2) * MXU_TILE
TM_PROJ = 512
TM_IN = 1024
TS_RET = 2048
ONES_ROWS = 16
ATT_TASKS_PER_BLOCK = 12


def _dot(a, b):
    return jnp.dot(a, b, preferred_element_type=F32)


def _dot_nt(a, b):
    return lax.dot_general(a, b, (((1,), (1,)), ((), ())), preferred_element_type=F32)


def _dot_tn(a, b):
    return lax.dot_general(a, b, (((0,), (0,)), ((), ())), preferred_element_type=F32)


def _layer_norm(z, w, b):
    mu = jnp.mean(z, axis=-1, keepdims=True)
    d = z - mu
    var = jnp.mean(d * d, axis=-1, keepdims=True)
    return d * lax.rsqrt(var + LN_EPS) * w + b


def _resident(shape):
    nd = len(shape)
    return pl.BlockSpec(shape, lambda *_: (0,) * nd, pipeline_mode=pl.Buffered(1))


def _ffn_ln_body(*refs, with_mixer):
    if with_mixer:
        x_ref, ret_ref, dif_ref, wo_ref, mlnw_ref, mlnb_ref, wg_ref, wu_ref, wd_ref, lnw_ref, lnb_ref, o_ref = refs
    else:
        x_ref, wg_ref, wu_ref, wd_ref, lnw_ref, lnb_ref, o_ref = refs
    part = x_ref.shape[0] // FFN_ROW_PARTS
    parts = [slice(r * part, (r + 1) * part) for r in range(FFN_ROW_PARTS)]

    def ffn_input(rows):
        x = x_ref[rows, :]
        if not with_mixer:
            return x
        mix = _dot(ret_ref[rows, :], wo_ref[0:RET_WIDTH, :]) + _dot(dif_ref[rows, :], wo_ref[RET_WIDTH:, :])
        return _layer_norm(DEEPNORM_ALPHA * x + mix, mlnw_ref[...], mlnb_ref[...])

    x_next = ffn_input(parts[0])
    for r, rows in enumerate(parts):
        x = x_next
        if r + 1 < len(parts):
            x_next = ffn_input(parts[r + 1])
        xb = x.astype(BF16)
        y = jnp.zeros(x.shape, F32)
        for sl in (slice(0, FF_SPLIT), slice(FF_SPLIT, D_FF)):
            g = _dot(xb, wg_ref[:, sl])
            u = _dot(xb, wu_ref[:, sl])
            h = (g * jax.nn.sigmoid(g) * u).astype(BF16)
            y = y + _dot(h, wd_ref[sl, :])
        z = DEEPNORM_ALPHA * x + 0.5 * y
        o_ref[rows, :] = _layer_norm(z, lnw_ref[...], lnb_ref[...])


def _ffn_ln(x, wg, wu, wd, lnw, lnb, mixer=None):
    n = x.shape[0]
    tm = TM_FFN
    rows = lambda width: pl.BlockSpec((tm, width), lambda i: (i, 0))
    vec = _resident((1, D_MODEL))
    operands, specs = [x], [rows(D_MODEL)]
    if mixer is not None:
        operands += list(mixer)
        specs += [rows(RET_WIDTH), rows(DIFF_WIDTH), _resident((RET_WIDTH + DIFF_WIDTH, D_MODEL)), vec, vec]
    operands += [wg, wu, wd, lnw, lnb]
    specs += [_resident((D_MODEL, D_FF)), _resident((D_MODEL, D_FF)), _resident((D_FF, D_MODEL)), vec, vec]
    return pl.pallas_call(
        functools.partial(_ffn_ln_body, with_mixer=mixer is not None),
        grid=(n // tm,),
        in_specs=specs,
        out_specs=rows(D_MODEL),
        out_shape=jax.ShapeDtypeStruct((n, D_MODEL), F32),
        compiler_params=pltpu.CompilerParams(
            dimension_semantics=("arbitrary",), vmem_limit_bytes=VMEM_LIMIT),
        name="mix_ffn_ln" if mixer is not None else "ffn_ln",
    )(*operands)


def _in_proj_body(x_ref, pos_ref, w_ref, inv_ref,
                  rq_ref, rk_ref, rv_ref, rg_ref, dqt_ref, dk_ref, dvt_ref):
    slabs = range(GROUP_WIDTH // LANES)
    for part in range(x_ref.shape[1] // TM_PROJ):
        rows = slice(part * TM_PROJ, (part + 1) * TM_PROJ)
        xb = x_ref[0, rows, :].astype(BF16)
        pos = pos_ref[0, :, rows].astype(F32)

        def tables(inv):
            ang = inv * pos
            c, s = jnp.cos(ang), jnp.sin(ang)
            return jnp.concatenate([c, c, c, c], axis=0).T, jnp.concatenate([-s, -s, s, s], axis=0).T

        def rotate(h, cs, scale):
            c, s = cs
            outs = []
            for j in slabs:
                hj = h[:, j * LANES:(j + 1) * LANES]
                r = hj * c + pltpu.roll(hj, LANES // 2, 1) * s
                outs.append(r if scale == 1.0 else r * scale)
            return outs

        def group(g):
            return _dot(xb, w_ref[:, g * GROUP_WIDTH:(g + 1) * GROUP_WIDTH])

        def store(ref, outs):
            ref[0, rows, :] = jnp.concatenate([o.astype(BF16) for o in outs], axis=1)

        def store_transposed(ref, outs):
            for j, o in enumerate(outs):
                ref[0, part, j * LANES:(j + 1) * LANES, :] = o.T.astype(BF16)

        h0 = group(0)
        ret_cs = tables(inv_ref[0])
        h1 = group(1)
        store(rq_ref, rotate(h0, ret_cs, 1.0))
        h2 = group(2)
        store(rk_ref, rotate(h1, ret_cs, HEAD_DIM ** -0.5))
        h3 = group(3)
        rv_ref[0, rows, :] = h2.astype(BF16)
        h4 = group(4)
        rg_ref[0, rows, :] = h3.astype(BF16)
        rope_cs = tables(inv_ref[1])
        h5 = group(5)
        store_transposed(dqt_ref, rotate(h4, rope_cs, HEAD_DIM ** -0.5 * LOG2_E))
        h6 = group(6)
        store(dk_ref, rotate(h5, rope_cs, 1.0))
        store_transposed(dvt_ref, [h6[:, j * LANES:(j + 1) * LANES] for j in slabs])


def _in_proj(x1, pos, w_in, inv):
    b, s, _ = x1.shape
    tm, tq = TM_IN, TM_PROJ
    row_spec = pl.BlockSpec((1, tm, GROUP_WIDTH), lambda bi, i: (bi, i, 0))
    col_spec = pl.BlockSpec((1, tm // tq, GROUP_WIDTH, tq), lambda bi, i: (bi, i, 0, 0))
    row_shape = jax.ShapeDtypeStruct((b, s, GROUP_WIDTH), BF16)
    col_shape = jax.ShapeDtypeStruct((b, s // tq, GROUP_WIDTH, tq), BF16)
    return pl.pallas_call(
        _in_proj_body,
        grid=(b, s // tm),
        in_specs=[
            pl.BlockSpec((1, tm, D_MODEL), lambda bi, i: (bi, i, 0)),
            pl.BlockSpec((1, 1, tm), lambda bi, i: (bi, 0, i)),
            _resident((D_MODEL, N_GROUPS * GROUP_WIDTH)),
            _resident((2, HEAD_DIM // 2, 1)),
        ],
        out_specs=[row_spec, row_spec, row_spec, row_spec, col_spec, row_spec, col_spec],
        out_shape=[row_shape, row_shape, row_shape, row_shape, col_shape, row_shape, col_shape],
        compiler_params=pltpu.CompilerParams(
            dimension_semantics=("arbitrary", "arbitrary"), vmem_limit_bytes=VMEM_LIMIT),
        name="in_proj",
    )(x1, pos, w_in, inv)


def _retention_body(q_ref, k_ref, v_ref, g_ref, nw_ref, intra_ref, qd_ref, kd_ref, cd_ref, avg_ref,
                    o_ref, state_ref, raw_ref):
    @pl.when(pl.program_id(1) == 0)
    def _():
        state_ref[...] = jnp.zeros(state_ref.shape, F32)

    lane = lax.broadcasted_iota(jnp.int32, (CHUNK, LANES), 1)
    row = lax.broadcasted_iota(jnp.int32, (CHUNK, LANES), 0)
    out_a = lane < HEAD_DIM
    head_a = (lane % HEAD_DIM) < (HEAD_DIM // 2)
    same_head = ((row % HEAD_DIM) < (HEAD_DIM // 2)) == out_a
    avg = avg_ref[...]
    pairs = [slice(p * LANES, (p + 1) * LANES) for p in range(RET_HEADS // 2)]

    states = [state_ref[p] for p in range(len(pairs))]
    for c in range(q_ref.shape[1] // CHUNK):
        rows = slice(c * CHUNK, (c + 1) * CHUNK)
        qs = [q_ref[0, rows, cols] for cols in pairs]
        ks = [k_ref[0, rows, cols] for cols in pairs]
        vs = [v_ref[0, rows, cols] for cols in pairs]
        zero = jnp.zeros_like(qs[0])
        sa = [_dot_nt(jnp.where(head_a, q, zero), k) for q, k in zip(qs, ks)]
        sb = [_dot_nt(jnp.where(head_a, zero, q), k) for q, k in zip(qs, ks)]
        cross = [_dot(q, s.astype(BF16)) for q, s in zip(qs, states)]
        kv = [_dot_tn((k.astype(F32) * kd_ref[p]).astype(BF16), v) for p, (k, v) in enumerate(zip(ks, vs))]
        pa = [(s * intra_ref[2 * p]).astype(BF16) for p, s in enumerate(sa)]
        pb = [(s * intra_ref[2 * p + 1]).astype(BF16) for p, s in enumerate(sb)]
        ia = [_dot(x, v) for x, v in zip(pa, vs)]
        ib = [_dot(x, v) for x, v in zip(pb, vs)]
        for p, cols in enumerate(pairs):
            states[p] = states[p] * cd_ref[p] + jnp.where(same_head, kv[p], 0.0)
            raw_ref[rows, cols] = jnp.where(out_a, ia[p], ib[p]) + cross[p] * qd_ref[p]
    for p in range(len(pairs)):
        state_ref[p] = states[p]

    def head_mean(t):
        hi = t.astype(BF16)
        lo = (t - hi.astype(F32)).astype(BF16)
        return _dot(jnp.concatenate([hi, lo], axis=1), avg)

    for cols in pairs:
        o = raw_ref[:, cols]
        d = o - head_mean(o)
        y = d * lax.rsqrt(head_mean(d * d) + NORM_EPS)
        gate = g_ref[0, :, cols].astype(F32)
        o_ref[0, :, cols] = (gate * jax.nn.sigmoid(gate) * (y * nw_ref[:, cols])).astype(BF16)


def _retention(rq, rk, rv, rg, nw, intra, qd, kd, cd, avg):
    b, s, _ = rq.shape
    ts = TS_RET
    seq_spec = pl.BlockSpec((1, ts, RET_WIDTH), lambda bi, si: (bi, si, 0))
    return pl.pallas_call(
        _retention_body,
        grid=(b, s // ts),
        in_specs=[seq_spec, seq_spec, seq_spec, seq_spec,
                  _resident((1, RET_WIDTH)),
                  _resident(intra.shape), _resident(qd.shape), _resident(kd.shape),
                  _resident(cd.shape), _resident(avg.shape)],
        out_specs=seq_spec,
        out_shape=jax.ShapeDtypeStruct((b, s, RET_WIDTH), BF16),
        scratch_shapes=[pltpu.VMEM((RET_HEADS // 2, LANES, LANES), F32),
                        pltpu.VMEM((ts, RET_WIDTH), F32)],
        compiler_params=pltpu.CompilerParams(
            dimension_semantics=("arbitrary", "arbitrary"), vmem_limit_bytes=VMEM_LIMIT),
        name="retention",
    )(rq, rk, rv, rg, nw, intra, qd, kd, cd, avg)


def _diff_attn_body(qt_ref, k_ref, vt_ref, lam_ref, nw_ref, o_ref, s0_ref, s1_ref, mx0_ref, mx1_ref, m_ref, acc_ref):
    nblk, _, tq = qt_ref.shape[1:]
    s_refs, mx_refs = (s0_ref, s1_ref), (mx0_ref, mx1_ref)
    row = lax.broadcasted_iota(jnp.int32, (LANES, 1), 0)
    first_head = (row % HEAD_DIM) < (HEAD_DIM // 2)
    heads = range(2)
    ones = jnp.ones((ONES_ROWS, tq), BF16)

    def scores(slot, task):
        i, t = task
        qt = qt_ref[0, i]
        zero = jnp.zeros_like(qt)
        k = k_ref[0, pl.ds(pl.multiple_of(t * tq, tq), tq), :]
        for h in heads:
            st = _dot(k, jnp.where(first_head, qt, zero) if h == 0 else jnp.where(first_head, zero, qt))
            s_refs[slot][h] = st
            mx_refs[slot][h] = jnp.max(st, axis=0, keepdims=True)

    def attend(slot, task):
        i, t = task
        vt = jnp.concatenate([vt_ref[0, t], ones], axis=0)
        for h in heads:
            m_old = m_ref[i, h]
            m_new = jnp.maximum(m_old, mx_refs[slot][h])
            p = jnp.exp2(s_refs[slot][h] - m_new)
            acc_ref[i, h] = jnp.exp2(m_old - m_new) * acc_ref[i, h] + _dot(vt, p.astype(BF16))
            m_ref[i, h] = m_new

    half = tq // 2
    early, late = slice(0, half), slice(half, tq)
    causal = (lax.broadcasted_iota(jnp.int32, (half, half), 0)
              <= lax.broadcasted_iota(jnp.int32, (half, half), 1))

    def scores_diagonal(slot, i):
        qt = qt_ref[0, i]
        zero = jnp.zeros_like(qt)
        k_early = k_ref[0, i * tq:i * tq + half, :]
        k_late = k_ref[0, i * tq + half:(i + 1) * tq, :]
        for h in heads:
            qh = jnp.where(first_head, qt, zero) if h == 0 else jnp.where(first_head, zero, qt)
            s_refs[slot][h, early, :] = _dot(k_early, qh)
            s_refs[slot][h, late, late] = _dot(k_late, qh[:, late])

    def attend_diagonal(slot, i):
        vt = jnp.concatenate([vt_ref[0, i], ones], axis=0)
        for h in heads:
            s_ee = jnp.where(causal, s_refs[slot][h, early, early], -jnp.inf)
            s_el = s_refs[slot][h, early, late]
            s_ll = jnp.where(causal, s_refs[slot][h, late, late], -jnp.inf)
            m_e = jnp.max(s_ee, axis=0, keepdims=True)
            m_l = jnp.maximum(jnp.max(s_el, axis=0, keepdims=True), jnp.max(s_ll, axis=0, keepdims=True))
            acc_ref[i, h, :, early] = _dot(vt[:, early], jnp.exp2(s_ee - m_e).astype(BF16))
            acc_ref[i, h, :, late] = (_dot(vt[:, early], jnp.exp2(s_el - m_l).astype(BF16))
                                      + _dot(vt[:, late], jnp.exp2(s_ll - m_l).astype(BF16)))
            m_ref[i, h] = jnp.concatenate([m_e, m_l], axis=1)

    scores_diagonal(0, 0)
    for i in range(nblk):
        if i + 1 < nblk:
            scores_diagonal((i + 1) % 2, i + 1)
        attend_diagonal(i % 2, i)

    def following(task):
        i, t = task
        last = t + 1 == i
        return jnp.where(last, jnp.minimum(i + 1, nblk - 1), i), jnp.where(last, 0, t + 1)

    def plain_block(_, task):
        for u in range(ATT_TASKS_PER_BLOCK):
            nxt = following(task)
            scores(1 - u % 2, nxt)
            attend(u % 2, task)
            task = nxt
        return task

    n_plain = nblk * (nblk - 1) // 2
    assert ATT_TASKS_PER_BLOCK % 2 == 0 and n_plain % ATT_TASKS_PER_BLOCK == 0
    first = (jnp.int32(1), jnp.int32(0))
    scores(0, first)
    lax.fori_loop(0, n_plain // ATT_TASKS_PER_BLOCK, plain_block, first)

    lam_v = lam_ref[...]
    lam = (jnp.exp(jnp.sum(lam_v[0:1] * lam_v[1:2], axis=-1, keepdims=True))
           - jnp.exp(jnp.sum(lam_v[2:3] * lam_v[3:4], axis=-1, keepdims=True)) + LAMBDA_INIT)

    def finish_tile(i, carry):
        num = [acc_ref[i, h, 0:LANES, :] for h in heads]
        den = [acc_ref[i, h, LANES:LANES + 1, :] for h in heads]
        o = (num[0] / den[0] - lam * (num[1] / den[1])).T
        o = o * lax.rsqrt(jnp.mean(o * o, axis=-1, keepdims=True) + NORM_EPS)
        o_ref[0, pl.ds(pl.multiple_of(i * tq, tq), tq), :] = (o * nw_ref[...] * (1.0 - LAMBDA_INIT)).astype(BF16)
        return carry

    lax.fori_loop(0, nblk, finish_tile, 0, unroll=True)


def _diff_attn(dqt, dk, dvt, lam_vecs, nw):
    b, nblk, _, tq = dqt.shape
    s = nblk * tq
    return pl.pallas_call(
        _diff_attn_body,
        grid=(b, DIFF_HEADS),
        in_specs=[
            pl.BlockSpec((1, nblk, LANES, tq), lambda bi, j: (bi, 0, j, 0)),
            pl.BlockSpec((1, s, LANES), lambda bi, j: (bi, 0, j)),
            pl.BlockSpec((1, nblk, LANES, tq), lambda bi, j: (bi, 0, j, 0)),
            _resident((4, HEAD_DIM)),
            pl.BlockSpec((1, LANES), lambda bi, j: (0, j)),
        ],
        out_specs=pl.BlockSpec((1, s, LANES), lambda bi, j: (bi, 0, j)),
        out_shape=jax.ShapeDtypeStruct((b, s, DIFF_WIDTH), BF16),
        scratch_shapes=[pltpu.VMEM((2, tq, tq), F32)] * 2
                       + [pltpu.VMEM((2, 1, tq), F32)] * 2
                       + [pltpu.VMEM((nblk, 2, 1, tq), F32),
                          pltpu.VMEM((nblk, 2, LANES + ONES_ROWS, tq), F32)],
        compiler_params=pltpu.CompilerParams(
            dimension_semantics=("arbitrary", "arbitrary"), vmem_limit_bytes=VMEM_LIMIT),
        name="diff_attn",
    )(dqt, dk, dvt, lam_vecs, nw)


def _rotation_inv_freqs():
    half = HEAD_DIM // 2
    f32 = np.float32
    inv_ret = f32(1.0) / np.power(f32(RET_THETA), np.linspace(0.0, 1.0, half, dtype=f32))
    inv_rope = f32(1.0) / np.power(f32(ROPE_THETA), np.arange(0, HEAD_DIM, 2, dtype=f32) / f32(HEAD_DIM))
    return np.stack([inv_ret, inv_rope]).astype(f32)[:, :, None]


def _retention_tables():
    h, f32 = RET_HEADS, np.float32
    log_g = np.log(f32(1.0) - np.power(f32(2.0), f32(-5.0) - np.arange(h, dtype=f32))).astype(f32)
    idx = np.arange(CHUNK, dtype=f32)
    rel = idx[:, None] - idx[None, :]
    intra = np.where(rel >= 0, np.exp(log_g[:, None, None] * np.maximum(rel, f32(0.0))), f32(0.0)).astype(f32)
    q_decay = np.exp(log_g[:, None] * (idx + f32(1.0))).astype(f32)
    k_decay = np.exp(log_g[:, None] * (f32(CHUNK - 1.0) - idx)).astype(f32)
    chunk_decay = np.exp(log_g * f32(CHUNK)).astype(f32)

    value_head = np.arange(LANES) // HEAD_DIM
    qk_head = (np.arange(LANES) % HEAD_DIM) // (HEAD_DIM // 2)

    def slab(t, head_of_lane):
        return np.ascontiguousarray(t.reshape(h // 2, 2, CHUNK)[:, head_of_lane, :].transpose(0, 2, 1))

    cd = chunk_decay.reshape(h // 2, 2)[:, qk_head]
    cd = np.ascontiguousarray(np.broadcast_to(cd[:, :, None], (h // 2, LANES, LANES)))
    avg = (value_head[:, None] == value_head[None, :]).astype(f32) / f32(HEAD_DIM)
    return intra, slab(q_decay, value_head), slab(k_decay, qk_head), cd, jnp.asarray(np.concatenate([avg, avg]), BF16)


def _slab_perm(w, even_odd):
    k, half, slabs = w.shape[0], HEAD_DIM // 2, GROUP_WIDTH // LANES
    if even_odd:
        t = w.reshape(k, slabs, 2, half, 2).transpose(0, 1, 4, 2, 3)
    else:
        t = w.reshape(k, slabs, 2, 2, half).transpose(0, 1, 3, 2, 4)
    return t.reshape(k, GROUP_WIDTH)


def kernel(x, positions, ffn1_w_gate, ffn1_w_up, ffn1_w_down, ln1_w, ln1_b, w_in, ret_norm_w,
           diff_lambda_q1, diff_lambda_k1, diff_lambda_q2, diff_lambda_k2, diff_norm_w, w_out,
           ln2_w, ln2_b, ffn2_w_gate, ffn2_w_up, ffn2_w_down, ln3_w, ln3_b):
    b, s, d = x.shape
    n = b * s
    l = 0
    xf = x.reshape(n, d)
    pos = positions.reshape(b, 1, s)

    groups = [w_in[l][:, g * GROUP_WIDTH:(g + 1) * GROUP_WIDTH].astype(BF16) for g in range(N_GROUPS)]
    for g, even_odd in ((0, True), (1, True), (4, False), (5, False)):
        groups[g] = _slab_perm(groups[g], even_odd)
    w_perm = jnp.concatenate(groups, axis=1)
    inv = _rotation_inv_freqs()
    intra, qd, kd, cd, avg = _retention_tables()
    lam_vecs = jnp.stack([diff_lambda_q1[l], diff_lambda_k1[l], diff_lambda_q2[l], diff_lambda_k2[l]])

    x1 = _ffn_ln(xf, ffn1_w_gate[l].astype(BF16), ffn1_w_up[l].astype(BF16), ffn1_w_down[l].astype(BF16),
                 ln1_w[l][None], ln1_b[l][None])
    rq, rk, rv, rg, dqt, dk, dvt = _in_proj(x1.reshape(b, s, d), pos, w_perm, inv)
    ret = _retention(rq, rk, rv, rg, ret_norm_w[l][None], intra, qd, kd, cd, avg)
    dif = _diff_attn(dqt, dk, dvt, lam_vecs, diff_norm_w[l][None])
    mixer = (ret.reshape(n, RET_WIDTH), dif.reshape(n, DIFF_WIDTH), w_out[l].astype(BF16),
             ln2_w[l][None], ln2_b[l][None])
    out = _ffn_ln(x1, ffn2_w_gate[l].astype(BF16), ffn2_w_up[l].astype(BF16), ffn2_w_down[l].astype(BF16),
                  ln3_w[l][None], ln3_b[l][None], mixer=mixer)
    return out.reshape(b, s, d)
```

```python
import functools
import math

import numpy as np
import jax
import jax.numpy as jnp
from jax import lax
from jax.experimental import pallas as pl
from jax.experimental.pallas import tpu as pltpu

F32 = jnp.float32
BF16 = jnp.bfloat16

D_MODEL = 1024
D_FF = 2816
HEAD_DIM = 64
RET_HEADS = 8
RET_WIDTH = RET_HEADS * HEAD_DIM
DIFF_HEADS = 4
DIFF_VDIM = 2 * HEAD_DIM
DIFF_WIDTH = DIFF_HEADS * DIFF_VDIM
GROUP_WIDTH = 512
N_GROUPS = 7
CHUNK = 128
ROPE_THETA = 10000.0
RET_THETA = 10000.0
LN_EPS = 1e-5
NORM_EPS = 1e-6
DEPTH = 1
DEEPNORM_ALPHA = (2.0 * DEPTH) ** 0.25
LAMBDA_INIT = 0.8 - 0.6 * math.exp(-0.3 * 0)
LOG2_E = math.log2(math.e)

LANES = 128
VMEM_LIMIT = 56 * 1024 * 1024

TM_FFN = 1024
FFN_ROW_PARTS = 4
MXU_TILE = 256
FF_SPLIT = (D_FF // MXU_TILE // 2) * MXU_TILE
TM_PROJ = 512
TM_IN = 1024
TS_RET = 2048
ONES_ROWS = 16
ATT_TASKS_PER_BLOCK = 12


def _dot(a, b):
    return jnp.dot(a, b, preferred_element_type=F32)


def _dot_nt(a, b):
    return lax.dot_general(a, b, (((1,), (1,)), ((), ())), preferred_element_type=F32)


def _dot_tn(a, b):
    return lax.dot_general(a, b, (((0,), (0,)), ((), ())), preferred_element_type=F32)


def _layer_norm(z, w, b):
    mu = jnp.mean(z, axis=-1, keepdims=True)
    d = z - mu
    var = jnp.mean(d * d, axis=-1, keepdims=True)
    return d * lax.rsqrt(var + LN_EPS) * w + b


def _resident(shape):
    nd = len(shape)
    return pl.BlockSpec(shape, lambda *_: (0,) * nd, pipeline_mode=pl.Buffered(1))


def _cast_rows(src_refs, dst_refs):
    for src, dst in zip(src_refs, dst_refs):
        dst[...] = src[...].astype(BF16)


def _cast_specs(arrays, steps, index):
    specs = [pl.BlockSpec((a.shape[0] // steps, a.shape[1]), index) for a in arrays]
    assert all(a.shape[0] % (16 * steps) == 0 for a in arrays)
    return specs, [jax.ShapeDtypeStruct(a.shape, BF16) for a in arrays]


def _ffn_ln_body(*refs, with_mixer, n_cast):
    n_main = 11 if with_mixer else 6
    _cast_rows(refs[n_main:n_main + n_cast], refs[n_main + n_cast + 1:])
    o_ref = refs[n_main + n_cast]
    if with_mixer:
        x_ref, ret_ref, dif_ref, wo_ref, mlnw_ref, mlnb_ref, wg_ref, wu_ref, wd_ref, lnw_ref, lnb_ref = refs[:n_main]
    else:
        x_ref, wg_ref, wu_ref, wd_ref, lnw_ref, lnb_ref = refs[:n_main]
    part = x_ref.shape[0] // FFN_ROW_PARTS
    parts = [slice(r * part, (r + 1) * part) for r in range(FFN_ROW_PARTS)]

    def ffn_input(rows):
        x = x_ref[rows, :]
        if not with_mixer:
            return x
        mix = _dot(ret_ref[rows, :], wo_ref[0:RET_WIDTH, :]) + _dot(dif_ref[rows, :], wo_ref[RET_WIDTH:, :])
        return _layer_norm(DEEPNORM_ALPHA * x + mix, mlnw_ref[...], mlnb_ref[...])

    x_next = ffn_input(parts[0])
    for r, rows in enumerate(parts):
        x = x_next
        if r + 1 < len(parts):
            x_next = ffn_input(parts[r + 1])
        xb = x.astype(BF16)
        y = jnp.zeros(x.shape, F32)
        for sl in (slice(0, FF_SPLIT), slice(FF_SPLIT, D_FF)):
            g = _dot(xb, wg_ref[:, sl])
            u = _dot(xb, wu_ref[:, sl])
            h = (g * jax.nn.sigmoid(g) * u).astype(BF16)
            y = y + _dot(h, wd_ref[sl, :])
        z = DEEPNORM_ALPHA * x + 0.5 * y
        o_ref[rows, :] = _layer_norm(z, lnw_ref[...], lnb_ref[...])


def _ffn_ln(x, wg, wu, wd, lnw, lnb, mixer=None, cast=()):
    n = x.shape[0]
    tm = TM_FFN
    cast_specs, cast_shapes = _cast_specs(cast, n // tm, lambda i: (i, 0))
    rows = lambda width: pl.BlockSpec((tm, width), lambda i: (i, 0))
    vec = _resident((1, D_MODEL))
    operands, specs = [x], [rows(D_MODEL)]
    if mixer is not None:
        operands += list(mixer)
        specs += [rows(RET_WIDTH), rows(DIFF_WIDTH), _resident((RET_WIDTH + DIFF_WIDTH, D_MODEL)), vec, vec]
    operands += [wg, wu, wd, lnw, lnb]
    specs += [_resident((D_MODEL, D_FF)), _resident((D_MODEL, D_FF)), _resident((D_FF, D_MODEL)), vec, vec]
    return pl.pallas_call(
        functools.partial(_ffn_ln_body, with_mixer=mixer is not None, n_cast=len(cast)),
        grid=(n // tm,),
        in_specs=specs + cast_specs,
        out_specs=[rows(D_MODEL)] + cast_specs,
        out_shape=[jax.ShapeDtypeStruct((n, D_MODEL), F32)] + cast_shapes,
        compiler_params=pltpu.CompilerParams(
            dimension_semantics=("arbitrary",), vmem_limit_bytes=VMEM_LIMIT),
        name="mix_ffn_ln" if mixer is not None else "ffn_ln",
    )(*operands, *cast)


def _in_proj_body(x_ref, pos_ref, w_ref, inv_ref, *refs):
    n_cast = (len(refs) - N_GROUPS) // 2
    _cast_rows(refs[:n_cast], refs[n_cast + N_GROUPS:])
    rq_ref, rk_ref, rv_ref, rg_ref, dqt_ref, dk_ref, dvt_ref = refs[n_cast:n_cast + N_GROUPS]
    slabs = range(GROUP_WIDTH // LANES)
    for part in range(x_ref.shape[1] // TM_PROJ):
        rows = slice(part * TM_PROJ, (part + 1) * TM_PROJ)
        xb = x_ref[0, rows, :].astype(BF16)
        pos = pos_ref[0, :, rows].astype(F32)

        def tables(inv):
            ang = inv * pos
            c, s = jnp.cos(ang), jnp.sin(ang)
            return jnp.concatenate([c, c, c, c], axis=0).T, jnp.concatenate([-s, -s, s, s], axis=0).T

        def rotate(h, cs, scale):
            c, s = cs
            outs = []
            for j in slabs:
                hj = h[:, j * LANES:(j + 1) * LANES]
                r = hj * c + pltpu.roll(hj, LANES // 2, 1) * s
                outs.append(r if scale == 1.0 else r * scale)
            return outs

        def group(g):
            return _dot(xb, w_ref[:, g * GROUP_WIDTH:(g + 1) * GROUP_WIDTH])

        def store(ref, outs):
            ref[0, rows, :] = jnp.concatenate([o.astype(BF16) for o in outs], axis=1)

        def store_transposed(ref, outs):
            for j, o in enumerate(outs):
                ref[0, part, j * LANES:(j + 1) * LANES, :] = o.T.astype(BF16)

        h0 = group(0)
        ret_cs = tables(inv_ref[0])
        h1 = group(1)
        store(rq_ref, rotate(h0, ret_cs, 1.0))
        h2 = group(2)
        store(rk_ref, rotate(h1, ret_cs, HEAD_DIM ** -0.5))
        h3 = group(3)
        rv_ref[0, rows, :] = h2.astype(BF16)
        h4 = group(4)
        rg_ref[0, rows, :] = h3.astype(BF16)
        rope_cs = tables(inv_ref[1])
        h5 = group(5)
        store_transposed(dqt_ref, rotate(h4, rope_cs, HEAD_DIM ** -0.5 * LOG2_E))
        h6 = group(6)
        store(dk_ref, rotate(h5, rope_cs, 1.0))
        store_transposed(dvt_ref, [h6[:, j * LANES:(j + 1) * LANES] for j in slabs])


def _in_proj(x1, pos, w_in, inv, cast=()):
    b, s, _ = x1.shape
    tm, tq = TM_IN, TM_PROJ
    cast_specs, cast_shapes = _cast_specs(cast, b * (s // tm), lambda bi, i: (bi * (s // tm) + i, 0))
    row_spec = pl.BlockSpec((1, tm, GROUP_WIDTH), lambda bi, i: (bi, i, 0))
    col_spec = pl.BlockSpec((1, tm // tq, GROUP_WIDTH, tq), lambda bi, i: (bi, i, 0, 0))
    row_shape = jax.ShapeDtypeStruct((b, s, GROUP_WIDTH), BF16)
    col_shape = jax.ShapeDtypeStruct((b, s // tq, GROUP_WIDTH, tq), BF16)
    return pl.pallas_call(
        _in_proj_body,
        grid=(b, s // tm),
        in_specs=[
            pl.BlockSpec((1, tm, D_MODEL), lambda bi, i: (bi, i, 0)),
            pl.BlockSpec((1, 1, tm), lambda bi, i: (bi, 0, i)),
            _resident((D_MODEL, N_GROUPS * GROUP_WIDTH)),
            _resident((2, HEAD_DIM // 2, 1)),
        ] + cast_specs,
        out_specs=[row_spec, row_spec, row_spec, row_spec, col_spec, row_spec, col_spec] + cast_specs,
        out_shape=[row_shape, row_shape, row_shape, row_shape, col_shape, row_shape, col_shape] + cast_shapes,
        compiler_params=pltpu.CompilerParams(
            dimension_semantics=("arbitrary", "arbitrary"), vmem_limit_bytes=VMEM_LIMIT),
        name="in_proj",
    )(x1, pos, w_in, inv, *cast)


def _retention_body(q_ref, k_ref, v_ref, g_ref, nw_ref, intra_ref, qd_ref, kd_ref, cd_ref, avg_ref,
                    o_ref, state_ref, raw_ref):
    @pl.when(pl.program_id(1) == 0)
    def _():
        state_ref[...] = jnp.zeros(state_ref.shape, F32)

    lane = lax.broadcasted_iota(jnp.int32, (CHUNK, LANES), 1)
    row = lax.broadcasted_iota(jnp.int32, (CHUNK, LANES), 0)
    out_a = lane < HEAD_DIM
    head_a = (lane % HEAD_DIM) < (HEAD_DIM // 2)
    same_head = ((row % HEAD_DIM) < (HEAD_DIM // 2)) == out_a
    avg = avg_ref[...]
    pairs = [slice(p * LANES, (p + 1) * LANES) for p in range(RET_HEADS // 2)]

    states = [state_ref[p] for p in range(len(pairs))]
    for c in range(q_ref.shape[1] // CHUNK):
        rows = slice(c * CHUNK, (c + 1) * CHUNK)
        qs = [q_ref[0, rows, cols] for cols in pairs]
        ks = [k_ref[0, rows, cols] for cols in pairs]
        vs = [v_ref[0, rows, cols] for cols in pairs]
        zero = jnp.zeros_like(qs[0])
        sa = [_dot_nt(jnp.where(head_a, q, zero), k) for q, k in zip(qs, ks)]
        sb = [_dot_nt(jnp.where(head_a, zero, q), k) for q, k in zip(qs, ks)]
        cross = [_dot(q, s.astype(BF16)) for q, s in zip(qs, states)]
        kv = [_dot_tn((k.astype(F32) * kd_ref[p]).astype(BF16), v) for p, (k, v) in enumerate(zip(ks, vs))]
        pa = [(s * intra_ref[2 * p]).astype(BF16) for p, s in enumerate(sa)]
        pb = [(s * intra_ref[2 * p + 1]).astype(BF16) for p, s in enumerate(sb)]
        ia = [_dot(x, v) for x, v in zip(pa, vs)]
        ib = [_dot(x, v) for x, v in zip(pb, vs)]
        for p, cols in enumerate(pairs):
            states[p] = states[p] * cd_ref[p] + jnp.where(same_head, kv[p], 0.0)
            raw_ref[rows, cols] = jnp.where(out_a, ia[p], ib[p]) + cross[p] * qd_ref[p]
    for p in range(len(pairs)):
        state_ref[p] = states[p]

    def head_mean(t):
        hi = t.astype(BF16)
        lo = (t - hi.astype(F32)).astype(BF16)
        return _dot(jnp.concatenate([hi, lo], axis=1), avg)

    for cols in pairs:
        o = raw_ref[:, cols]
        d = o - head_mean(o)
        y = d * lax.rsqrt(head_mean(d * d) + NORM_EPS)
        gate = g_ref[0, :, cols].astype(F32)
        o_ref[0, :, cols] = (gate * jax.nn.sigmoid(gate) * (y * nw_ref[:, cols])).astype(BF16)


def _retention(rq, rk, rv, rg, nw, intra, qd, kd, cd, avg):
    b, s, _ = rq.shape
    ts = TS_RET
    seq_spec = pl.BlockSpec((1, ts, RET_WIDTH), lambda bi, si: (bi, si, 0))
    return pl.pallas_call(
        _retention_body,
        grid=(b, s // ts),
        in_specs=[seq_spec, seq_spec, seq_spec, seq_spec,
                  _resident((1, RET_WIDTH)),
                  _resident(intra.shape), _resident(qd.shape), _resident(kd.shape),
                  _resident(cd.shape), _resident(avg.shape)],
        out_specs=seq_spec,
        out_shape=jax.ShapeDtypeStruct((b, s, RET_WIDTH), BF16),
        scratch_shapes=[pltpu.VMEM((RET_HEADS // 2, LANES, LANES), F32),
                        pltpu.VMEM((ts, RET_WIDTH), F32)],
        compiler_params=pltpu.CompilerParams(
            dimension_semantics=("arbitrary", "arbitrary"), vmem_limit_bytes=VMEM_LIMIT),
        name="retention",
    )(rq, rk, rv, rg, nw, intra, qd, kd, cd, avg)


def _diff_attn_body(qt_ref, k_ref, vt_ref, lam_ref, nw_ref, o_ref, s0_ref, s1_ref, mx0_ref, mx1_ref, m_ref, acc_ref):
    nblk, _, tq = qt_ref.shape[1:]
    s_refs, mx_refs = (s0_ref, s1_ref), (mx0_ref, mx1_ref)
    row = lax.broadcasted_iota(jnp.int32, (LANES, 1), 0)
    first_head = (row % HEAD_DIM) < (HEAD_DIM // 2)
    heads = range(2)
    ones = jnp.ones((ONES_ROWS, tq), BF16)

    def scores(slot, task):
        i, t = task
        qt = qt_ref[0, i]
        zero = jnp.zeros_like(qt)
        k = k_ref[0, pl.ds(pl.multiple_of(t * tq, tq), tq), :]
        for h in heads:
            st = _dot(k, jnp.where(first_head, qt, zero) if h == 0 else jnp.where(first_head, zero, qt))
            s_refs[slot][h] = st
            mx_refs[slot][h] = jnp.max(st, axis=0, keepdims=True)

    def attend(slot, task):
        i, t = task
        vt = jnp.concatenate([vt_ref[0, t], ones], axis=0)
        for h in heads:
            m_old = m_ref[i, h]
            m_new = jnp.maximum(m_old, mx_refs[slot][h])
            p = jnp.exp2(s_refs[slot][h] - m_new)
            acc_ref[i, h] = jnp.exp2(m_old - m_new) * acc_ref[i, h] + _dot(vt, p.astype(BF16))
            m_ref[i, h] = m_new

    half = tq // 2
    early, late = slice(0, half), slice(half, tq)
    causal = (lax.broadcasted_iota(jnp.int32, (half, half), 0)
              <= lax.broadcasted_iota(jnp.int32, (half, half), 1))

    def scores_diagonal(slot, i):
        qt = qt_ref[0, i]
        zero = jnp.zeros_like(qt)
        k_early = k_ref[0, i * tq:i * tq + half, :]
        k_late = k_ref[0, i * tq + half:(i + 1) * tq, :]
        for h in heads:
            qh = jnp.where(first_head, qt, zero) if h == 0 else jnp.where(first_head, zero, qt)
            s_refs[slot][h, early, :] = _dot(k_early, qh)
            s_refs[slot][h, late, late] = _dot(k_late, qh[:, late])

    def attend_diagonal(slot, i):
        vt = jnp.concatenate([vt_ref[0, i], ones], axis=0)
        for h in heads:
            s_ee = jnp.where(causal, s_refs[slot][h, early, early], -jnp.inf)
            s_el = s_refs[slot][h, early, late]
            s_ll = jnp.where(causal, s_refs[slot][h, late, late], -jnp.inf)
            m_e = jnp.max(s_ee, axis=0, keepdims=True)
            m_l = jnp.maximum(jnp.max(s_el, axis=0, keepdims=True), jnp.max(s_ll, axis=0, keepdims=True))
            acc_ref[i, h, :, early] = _dot(vt[:, early], jnp.exp2(s_ee - m_e).astype(BF16))
            acc_ref[i, h, :, late] = (_dot(vt[:, early], jnp.exp2(s_el - m_l).astype(BF16))
                                      + _dot(vt[:, late], jnp.exp2(s_ll - m_l).astype(BF16)))
            m_ref[i, h] = jnp.concatenate([m_e, m_l], axis=1)

    scores_diagonal(0, 0)
    for i in range(nblk):
        if i + 1 < nblk:
            scores_diagonal((i + 1) % 2, i + 1)
        attend_diagonal(i % 2, i)

    def following(task):
        i, t = task
        last = t + 1 == i
        return jnp.where(last, jnp.minimum(i + 1, nblk - 1), i), jnp.where(last, 0, t + 1)

    def plain_block(_, task):
        for u in range(ATT_TASKS_PER_BLOCK):
            nxt = following(task)
            scores(1 - u % 2, nxt)
            attend(u % 2, task)
            task = nxt
        return task

    n_plain = nblk * (nblk - 1) // 2
    assert ATT_TASKS_PER_BLOCK % 2 == 0 and n_plain % ATT_TASKS_PER_BLOCK == 0
    first = (jnp.int32(1), jnp.int32(0))
    scores(0, first)
    lax.fori_loop(0, n_plain // ATT_TASKS_PER_BLOCK, plain_block, first)

    lam_v = lam_ref[...]
    lam = (jnp.exp(jnp.sum(lam_v[0:1] * lam_v[1:2], axis=-1, keepdims=True))
           - jnp.exp(jnp.sum(lam_v[2:3] * lam_v[3:4], axis=-1, keepdims=True)) + LAMBDA_INIT)

    def finish_tile(i, carry):
        num = [acc_ref[i, h, 0:LANES, :] for h in heads]
        den = [acc_ref[i, h, LANES:LANES + 1, :] for h in heads]
        o = (num[0] / den[0] - lam * (num[1] / den[1])).T
        o = o * lax.rsqrt(jnp.mean(o * o, axis=-1, keepdims=True) + NORM_EPS)
        o_ref[0, pl.ds(pl.multiple_of(i * tq, tq), tq), :] = (o * nw_ref[...] * (1.0 - LAMBDA_INIT)).astype(BF16)
        return carry

    lax.fori_loop(0, nblk, finish_tile, 0, unroll=True)


def _diff_attn(dqt, dk, dvt, lam_vecs, nw):
    b, nblk, _, tq = dqt.shape
    s = nblk * tq
    return pl.pallas_call(
        _diff_attn_body,
        grid=(b, DIFF_HEADS),
        in_specs=[
            pl.BlockSpec((1, nblk, LANES, tq), lambda bi, j: (bi, 0, j, 0)),
            pl.BlockSpec((1, s, LANES), lambda bi, j: (bi, 0, j)),
            pl.BlockSpec((1, nblk, LANES, tq), lambda bi, j: (bi, 0, j, 0)),
            _resident((4, HEAD_DIM)),
            pl.BlockSpec((1, LANES), lambda bi, j: (0, j)),
        ],
        out_specs=pl.BlockSpec((1, s, LANES), lambda bi, j: (bi, 0, j)),
        out_shape=jax.ShapeDtypeStruct((b, s, DIFF_WIDTH), BF16),
        scratch_shapes=[pltpu.VMEM((2, tq, tq), F32)] * 2
                       + [pltpu.VMEM((2, 1, tq), F32)] * 2
                       + [pltpu.VMEM((nblk, 2, 1, tq), F32),
                          pltpu.VMEM((nblk, 2, LANES + ONES_ROWS, tq), F32)],
        compiler_params=pltpu.CompilerParams(
            dimension_semantics=("arbitrary", "arbitrary"), vmem_limit_bytes=VMEM_LIMIT),
        name="diff_attn",
    )(dqt, dk, dvt, lam_vecs, nw)


def _rotation_inv_freqs():
    half = HEAD_DIM // 2
    f32 = np.float32
    inv_ret = f32(1.0) / np.power(f32(RET_THETA), np.linspace(0.0, 1.0, half, dtype=f32))
    inv_rope = f32(1.0) / np.power(f32(ROPE_THETA), np.arange(0, HEAD_DIM, 2, dtype=f32) / f32(HEAD_DIM))
    return np.stack([inv_ret, inv_rope]).astype(f32)[:, :, None]


def _retention_tables():
    h, f32 = RET_HEADS, np.float32
    log_g = np.log(f32(1.0) - np.power(f32(2.0), f32(-5.0) - np.arange(h, dtype=f32))).astype(f32)
    idx = np.arange(CHUNK, dtype=f32)
    rel = idx[:, None] - idx[None, :]
    intra = np.where(rel >= 0, np.exp(log_g[:, None, None] * np.maximum(rel, f32(0.0))), f32(0.0)).astype(f32)
    q_decay = np.exp(log_g[:, None] * (idx + f32(1.0))).astype(f32)
    k_decay = np.exp(log_g[:, None] * (f32(CHUNK - 1.0) - idx)).astype(f32)
    chunk_decay = np.exp(log_g * f32(CHUNK)).astype(f32)

    value_head = np.arange(LANES) // HEAD_DIM
    qk_head = (np.arange(LANES) % HEAD_DIM) // (HEAD_DIM // 2)

    def slab(t, head_of_lane):
        return np.ascontiguousarray(t.reshape(h // 2, 2, CHUNK)[:, head_of_lane, :].transpose(0, 2, 1))

    cd = chunk_decay.reshape(h // 2, 2)[:, qk_head]
    cd = np.ascontiguousarray(np.broadcast_to(cd[:, :, None], (h // 2, LANES, LANES)))
    avg = (value_head[:, None] == value_head[None, :]).astype(f32) / f32(HEAD_DIM)
    return intra, slab(q_decay, value_head), slab(k_decay, qk_head), cd, jnp.asarray(np.concatenate([avg, avg]), BF16)


def _slab_perm(w, even_odd):
    k, half, slabs = w.shape[0], HEAD_DIM // 2, GROUP_WIDTH // LANES
    if even_odd:
        t = w.reshape(k, slabs, 2, half, 2).transpose(0, 1, 4, 2, 3)
    else:
        t = w.reshape(k, slabs, 2, 2, half).transpose(0, 1, 3, 2, 4)
    return t.reshape(k, GROUP_WIDTH)


def kernel(x, positions, ffn1_w_gate, ffn1_w_up, ffn1_w_down, ln1_w, ln1_b, w_in, ret_norm_w,
           diff_lambda_q1, diff_lambda_k1, diff_lambda_q2, diff_lambda_k2, diff_norm_w, w_out,
           ln2_w, ln2_b, ffn2_w_gate, ffn2_w_up, ffn2_w_down, ln3_w, ln3_b):
    b, s, d = x.shape
    n = b * s
    l = 0
    xf = x.reshape(n, d)
    pos = positions.reshape(b, 1, s)

    inv = _rotation_inv_freqs()
    intra, qd, kd, cd, avg = _retention_tables()
    lam_vecs = jnp.stack([diff_lambda_q1[l], diff_lambda_k1[l], diff_lambda_q2[l], diff_lambda_k2[l]])

    x1, w_in_bf16 = _ffn_ln(xf, ffn1_w_gate[l].astype(BF16), ffn1_w_up[l].astype(BF16), ffn1_w_down[l].astype(BF16),
                            ln1_w[l][None], ln1_b[l][None], cast=(w_in[l],))
    groups = [w_in_bf16[:, g * GROUP_WIDTH:(g + 1) * GROUP_WIDTH] for g in range(N_GROUPS)]
    for g, even_odd in ((0, True), (1, True), (4, False), (5, False)):
        groups[g] = _slab_perm(groups[g], even_odd)
    later = (ffn2_w_gate[l], ffn2_w_up[l], ffn2_w_down[l].reshape(D_MODEL, D_FF), w_out[l])
    rq, rk, rv, rg, dqt, dk, dvt, wg2, wu2, wd2, wo = _in_proj(
        x1.reshape(b, s, d), pos, jnp.concatenate(groups, axis=1), inv, cast=later)
    ret = _retention(rq, rk, rv, rg, ret_norm_w[l][None], intra, qd, kd, cd, avg)
    dif = _diff_attn(dqt, dk, dvt, lam_vecs, diff_norm_w[l][None])
    mixer = (ret.reshape(n, RET_WIDTH), dif.reshape(n, DIFF_WIDTH), wo, ln2_w[l][None], ln2_b[l][None])
    (out,) = _ffn_ln(x1, wg2, wu2, wd2.reshape(D_FF, D_MODEL), ln3_w[l][None], ln3_b[l][None], mixer=mixer)
    return out.reshape(b, s, d)
```

```python
import functools
import math

import numpy as np
import jax
import jax.numpy as jnp
from jax import lax
from jax.experimental import pallas as pl
from jax.experimental.pallas import tpu as pltpu

F32 = jnp.float32
BF16 = jnp.bfloat16

D_MODEL = 1024
D_FF = 2816
HEAD_DIM = 64
RET_HEADS = 8
RET_WIDTH = RET_HEADS * HEAD_DIM
DIFF_HEADS = 4
DIFF_VDIM = 2 * HEAD_DIM
DIFF_WIDTH = DIFF_HEADS * DIFF_VDIM
GROUP_WIDTH = 512
N_GROUPS = 7
CHUNK = 128
ROPE_THETA = 10000.0
RET_THETA = 10000.0
LN_EPS = 1e-5
NORM_EPS = 1e-6
DEPTH = 1
DEEPNORM_ALPHA = (2.0 * DEPTH) ** 0.25
LAMBDA_INIT = 0.8 - 0.6 * math.exp(-0.3 * 0)
LOG2_E = math.log2(math.e)

LANES = 128
VMEM_LIMIT = 56 * 1024 * 1024

TM_FFN = 1024
FFN_ROW_PARTS = 4
MXU_TILE = 256
FF_SPLIT = (D_FF // MXU_TILE // 2) * MXU_TILE
TM_PROJ = 512
TM_IN = 1024
TS_RET = 2048
ONES_ROWS = 16
ATT_TASKS_PER_BLOCK = 12


def _dot(a, b):
    return jnp.dot(a, b, preferred_element_type=F32)


def _dot_nt(a, b):
    return lax.dot_general(a, b, (((1,), (1,)), ((), ())), preferred_element_type=F32)


def _dot_tn(a, b):
    return lax.dot_general(a, b, (((0,), (0,)), ((), ())), preferred_element_type=F32)


def _layer_norm(z, w, b):
    mu = jnp.mean(z, axis=-1, keepdims=True)
    d = z - mu
    var = jnp.mean(d * d, axis=-1, keepdims=True)
    return d * lax.rsqrt(var + LN_EPS) * w + b


def _resident(shape):
    nd = len(shape)
    return pl.BlockSpec(shape, lambda *_: (0,) * nd, pipeline_mode=pl.Buffered(1))


def _cast_rows(src_refs, dst_refs):
    for src, dst in zip(src_refs, dst_refs):
        dst[...] = src[0].astype(BF16)


def _cast_specs(arrays, steps, step_of):
    assert all(a.shape[0] == 1 and a.shape[1] % (16 * steps) == 0 for a in arrays)
    in_specs = [pl.BlockSpec((1, a.shape[1] // steps, a.shape[2]), lambda *g: (0, step_of(*g), 0)) for a in arrays]
    out_specs = [pl.BlockSpec((a.shape[1] // steps, a.shape[2]), lambda *g: (step_of(*g), 0)) for a in arrays]
    return in_specs, out_specs, [jax.ShapeDtypeStruct(a.shape[1:], BF16) for a in arrays]


def _ffn_ln_body(*refs, with_mixer, n_cast):
    n_main = 11 if with_mixer else 6
    _cast_rows(refs[n_main:n_main + n_cast], refs[n_main + n_cast + 1:])
    o_ref = refs[n_main + n_cast]
    if with_mixer:
        x_ref, ret_ref, dif_ref, wo_ref, mlnw_ref, mlnb_ref, wg_ref, wu_ref, wd_ref, lnw_ref, lnb_ref = refs[:n_main]
    else:
        x_ref, wg_ref, wu_ref, wd_ref, lnw_ref, lnb_ref = refs[:n_main]
    part = x_ref.shape[0] // FFN_ROW_PARTS
    parts = [slice(r * part, (r + 1) * part) for r in range(FFN_ROW_PARTS)]

    def ffn_input(rows):
        x = x_ref[rows, :]
        if not with_mixer:
            return x
        mix = _dot(ret_ref[rows, :], wo_ref[0:RET_WIDTH, :]) + _dot(dif_ref[rows, :], wo_ref[RET_WIDTH:, :])
        return _layer_norm(DEEPNORM_ALPHA * x + mix, mlnw_ref[...], mlnb_ref[...])

    x_next = ffn_input(parts[0])
    for r, rows in enumerate(parts):
        x = x_next
        if r + 1 < len(parts):
            x_next = ffn_input(parts[r + 1])
        xb = x.astype(BF16)
        y = jnp.zeros(x.shape, F32)
        for sl in (slice(0, FF_SPLIT), slice(FF_SPLIT, D_FF)):
            g = _dot(xb, wg_ref[:, sl])
            u = _dot(xb, wu_ref[:, sl])
            h = (g * jax.nn.sigmoid(g) * u).astype(BF16)
            y = y + _dot(h, wd_ref[sl, :])
        z = DEEPNORM_ALPHA * x + 0.5 * y
        o_ref[rows, :] = _layer_norm(z, lnw_ref[...], lnb_ref[...])


def _ffn_ln(x, wg, wu, wd, lnw, lnb, mixer=None, cast=()):
    n = x.shape[0]
    tm = TM_FFN
    cast_in, cast_out, cast_shapes = _cast_specs(cast, n // tm, lambda i: i)
    rows = lambda width: pl.BlockSpec((tm, width), lambda i: (i, 0))
    vec = _resident((1, D_MODEL))
    operands, specs = [x], [rows(D_MODEL)]
    if mixer is not None:
        operands += list(mixer)
        specs += [rows(RET_WIDTH), rows(DIFF_WIDTH), _resident((RET_WIDTH + DIFF_WIDTH, D_MODEL)), vec, vec]
    operands += [wg, wu, wd, lnw, lnb]
    specs += [_resident((D_MODEL, D_FF)), _resident((D_MODEL, D_FF)), _resident((D_FF, D_MODEL)), vec, vec]
    return pl.pallas_call(
        functools.partial(_ffn_ln_body, with_mixer=mixer is not None, n_cast=len(cast)),
        grid=(n // tm,),
        in_specs=specs + cast_in,
        out_specs=[rows(D_MODEL)] + cast_out,
        out_shape=[jax.ShapeDtypeStruct((n, D_MODEL), F32)] + cast_shapes,
        compiler_params=pltpu.CompilerParams(
            dimension_semantics=("arbitrary",), vmem_limit_bytes=VMEM_LIMIT),
        name="mix_ffn_ln" if mixer is not None else "ffn_ln",
    )(*operands, *cast)


def _in_proj_body(x_ref, pos_ref, w_ref, inv_ref, *refs):
    n_cast = (len(refs) - N_GROUPS) // 2
    _cast_rows(refs[:n_cast], refs[n_cast + N_GROUPS:])
    rq_ref, rk_ref, rv_ref, rg_ref, dqt_ref, dk_ref, dvt_ref = refs[n_cast:n_cast + N_GROUPS]
    slabs = range(GROUP_WIDTH // LANES)
    for part in range(x_ref.shape[1] // TM_PROJ):
        rows = slice(part * TM_PROJ, (part + 1) * TM_PROJ)
        xb = x_ref[0, rows, :].astype(BF16)
        pos = pos_ref[0, :, rows].astype(F32)

        def tables(inv):
            ang = inv * pos
            c, s = jnp.cos(ang), jnp.sin(ang)
            return jnp.concatenate([c, c, c, c], axis=0).T, jnp.concatenate([-s, -s, s, s], axis=0).T

        def rotate(h, cs, scale):
            c, s = cs
            outs = []
            for j in slabs:
                hj = h[:, j * LANES:(j + 1) * LANES]
                r = hj * c + pltpu.roll(hj, LANES // 2, 1) * s
                outs.append(r if scale == 1.0 else r * scale)
            return outs

        def group(g):
            return _dot(xb, w_ref[:, g * GROUP_WIDTH:(g + 1) * GROUP_WIDTH])

        def store(ref, outs):
            ref[0, rows, :] = jnp.concatenate([o.astype(BF16) for o in outs], axis=1)

        def store_transposed(ref, outs):
            for j, o in enumerate(outs):
                ref[0, part, j * LANES:(j + 1) * LANES, :] = o.T.astype(BF16)

        h0 = group(0)
        ret_cs = tables(inv_ref[0])
        h1 = group(1)
        store(rq_ref, rotate(h0, ret_cs, 1.0))
        h2 = group(2)
        store(rk_ref, rotate(h1, ret_cs, HEAD_DIM ** -0.5))
        h3 = group(3)
        rv_ref[0, rows, :] = h2.astype(BF16)
        h4 = group(4)
        rg_ref[0, rows, :] = h3.astype(BF16)
        rope_cs = tables(inv_ref[1])
        h5 = group(5)
        store_transposed(dqt_ref, rotate(h4, rope_cs, HEAD_DIM ** -0.5 * LOG2_E))
        h6 = group(6)
        store(dk_ref, rotate(h5, rope_cs, 1.0))
        store_transposed(dvt_ref, [h6[:, j * LANES:(j + 1) * LANES] for j in slabs])


def _in_proj(x1, pos, w_in, inv, cast=()):
    b, s, _ = x1.shape
    tm, tq = TM_IN, TM_PROJ
    cast_in, cast_out, cast_shapes = _cast_specs(cast, b * (s // tm), lambda bi, i: bi * (s // tm) + i)
    row_spec = pl.BlockSpec((1, tm, GROUP_WIDTH), lambda bi, i: (bi, i, 0))
    col_spec = pl.BlockSpec((1, tm // tq, GROUP_WIDTH, tq), lambda bi, i: (bi, i, 0, 0))
    row_shape = jax.ShapeDtypeStruct((b, s, GROUP_WIDTH), BF16)
    col_shape = jax.ShapeDtypeStruct((b, s // tq, GROUP_WIDTH, tq), BF16)
    return pl.pallas_call(
        _in_proj_body,
        grid=(b, s // tm),
        in_specs=[
            pl.BlockSpec((1, tm, D_MODEL), lambda bi, i: (bi, i, 0)),
            pl.BlockSpec((1, 1, tm), lambda bi, i: (bi, 0, i)),
            _resident((D_MODEL, N_GROUPS * GROUP_WIDTH)),
            _resident((2, HEAD_DIM // 2, 1)),
        ] + cast_in,
        out_specs=[row_spec, row_spec, row_spec, row_spec, col_spec, row_spec, col_spec] + cast_out,
        out_shape=[row_shape, row_shape, row_shape, row_shape, col_shape, row_shape, col_shape] + cast_shapes,
        compiler_params=pltpu.CompilerParams(
            dimension_semantics=("arbitrary", "arbitrary"), vmem_limit_bytes=VMEM_LIMIT),
        name="in_proj",
    )(x1, pos, w_in, inv, *cast)


def _retention_body(q_ref, k_ref, v_ref, g_ref, nw_ref, intra_ref, qd_ref, kd_ref, cd_ref, avg_ref, *refs):
    n_cast = (len(refs) - 3) // 2
    o_ref, state_ref, raw_ref = refs[n_cast], refs[-2], refs[-1]
    _cast_rows(refs[:n_cast], refs[n_cast + 1:-2])
    @pl.when(pl.program_id(1) == 0)
    def _():
        state_ref[...] = jnp.zeros(state_ref.shape, F32)

    lane = lax.broadcasted_iota(jnp.int32, (CHUNK, LANES), 1)
    row = lax.broadcasted_iota(jnp.int32, (CHUNK, LANES), 0)
    out_a = lane < HEAD_DIM
    head_a = (lane % HEAD_DIM) < (HEAD_DIM // 2)
    same_head = ((row % HEAD_DIM) < (HEAD_DIM // 2)) == out_a
    avg = avg_ref[...]
    pairs = [slice(p * LANES, (p + 1) * LANES) for p in range(RET_HEADS // 2)]

    states = [state_ref[p] for p in range(len(pairs))]
    for c in range(q_ref.shape[1] // CHUNK):
        rows = slice(c * CHUNK, (c + 1) * CHUNK)
        qs = [q_ref[0, rows, cols] for cols in pairs]
        ks = [k_ref[0, rows, cols] for cols in pairs]
        vs = [v_ref[0, rows, cols] for cols in pairs]
        zero = jnp.zeros_like(qs[0])
        sa = [_dot_nt(jnp.where(head_a, q, zero), k) for q, k in zip(qs, ks)]
        sb = [_dot_nt(jnp.where(head_a, zero, q), k) for q, k in zip(qs, ks)]
        cross = [_dot(q, s.astype(BF16)) for q, s in zip(qs, states)]
        kv = [_dot_tn((k.astype(F32) * kd_ref[p]).astype(BF16), v) for p, (k, v) in enumerate(zip(ks, vs))]
        pa = [(s * intra_ref[2 * p]).astype(BF16) for p, s in enumerate(sa)]
        pb = [(s * intra_ref[2 * p + 1]).astype(BF16) for p, s in enumerate(sb)]
        ia = [_dot(x, v) for x, v in zip(pa, vs)]
        ib = [_dot(x, v) for x, v in zip(pb, vs)]
        for p, cols in enumerate(pairs):
            states[p] = states[p] * cd_ref[p] + jnp.where(same_head, kv[p], 0.0)
            raw_ref[rows, cols] = jnp.where(out_a, ia[p], ib[p]) + cross[p] * qd_ref[p]
    for p in range(len(pairs)):
        state_ref[p] = states[p]

    def head_mean(t):
        hi = t.astype(BF16)
        lo = (t - hi.astype(F32)).astype(BF16)
        return _dot(jnp.concatenate([hi, lo], axis=1), avg)

    for cols in pairs:
        o = raw_ref[:, cols]
        d = o - head_mean(o)
        y = d * lax.rsqrt(head_mean(d * d) + NORM_EPS)
        gate = g_ref[0, :, cols].astype(F32)
        o_ref[0, :, cols] = (gate * jax.nn.sigmoid(gate) * (y * nw_ref[:, cols])).astype(BF16)


def _retention(rq, rk, rv, rg, nw, intra, qd, kd, cd, avg, cast=()):
    b, s, _ = rq.shape
    ts = TS_RET
    cast_in, cast_out, cast_shapes = _cast_specs(cast, b * (s // ts), lambda bi, si: bi * (s // ts) + si)
    seq_spec = pl.BlockSpec((1, ts, RET_WIDTH), lambda bi, si: (bi, si, 0))
    return pl.pallas_call(
        _retention_body,
        grid=(b, s // ts),
        in_specs=[seq_spec, seq_spec, seq_spec, seq_spec,
                  _resident((1, RET_WIDTH)),
                  _resident(intra.shape), _resident(qd.shape), _resident(kd.shape),
                  _resident(cd.shape), _resident(avg.shape)] + cast_in,
        out_specs=[seq_spec] + cast_out,
        out_shape=[jax.ShapeDtypeStruct((b, s, RET_WIDTH), BF16)] + cast_shapes,
        scratch_shapes=[pltpu.VMEM((RET_HEADS // 2, LANES, LANES), F32),
                        pltpu.VMEM((ts, RET_WIDTH), F32)],
        compiler_params=pltpu.CompilerParams(
            dimension_semantics=("arbitrary", "arbitrary"), vmem_limit_bytes=VMEM_LIMIT),
        name="retention",
    )(rq, rk, rv, rg, nw, intra, qd, kd, cd, avg, *cast)


def _diff_attn_body(qt_ref, k_ref, vt_ref, lam_ref, nw_ref, o_ref, s0_ref, s1_ref, mx0_ref, mx1_ref, m_ref, acc_ref):
    nblk, _, tq = qt_ref.shape[1:]
    s_refs, mx_refs = (s0_ref, s1_ref), (mx0_ref, mx1_ref)
    row = lax.broadcasted_iota(jnp.int32, (LANES, 1), 0)
    first_head = (row % HEAD_DIM) < (HEAD_DIM // 2)
    heads = range(2)
    ones = jnp.ones((ONES_ROWS, tq), BF16)

    def scores(slot, task):
        i, t = task
        qt = qt_ref[0, i]
        zero = jnp.zeros_like(qt)
        k = k_ref[0, pl.ds(pl.multiple_of(t * tq, tq), tq), :]
        for h in heads:
            st = _dot(k, jnp.where(first_head, qt, zero) if h == 0 else jnp.where(first_head, zero, qt))
            s_refs[slot][h] = st
            mx_refs[slot][h] = jnp.max(st, axis=0, keepdims=True)

    def attend(slot, task):
        i, t = task
        vt = jnp.concatenate([vt_ref[0, t], ones], axis=0)
        for h in heads:
            m_old = m_ref[i, h]
            m_new = jnp.maximum(m_old, mx_refs[slot][h])
            p = jnp.exp2(s_refs[slot][h] - m_new)
            acc_ref[i, h] = jnp.exp2(m_old - m_new) * acc_ref[i, h] + _dot(vt, p.astype(BF16))
            m_ref[i, h] = m_new

    half = tq // 2
    early, late = slice(0, half), slice(half, tq)
    causal = (lax.broadcasted_iota(jnp.int32, (half, half), 0)
              <= lax.broadcasted_iota(jnp.int32, (half, half), 1))

    def scores_diagonal(slot, i):
        qt = qt_ref[0, i]
        zero = jnp.zeros_like(qt)
        k_early = k_ref[0, i * tq:i * tq + half, :]
        k_late = k_ref[0, i * tq + half:(i + 1) * tq, :]
        for h in heads:
            qh = jnp.where(first_head, qt, zero) if h == 0 else jnp.where(first_head, zero, qt)
            s_refs[slot][h, early, :] = _dot(k_early, qh)
            s_refs[slot][h, late, late] = _dot(k_late, qh[:, late])

    def attend_diagonal(slot, i):
        vt = jnp.concatenate([vt_ref[0, i], ones], axis=0)
        for h in heads:
            s_ee = jnp.where(causal, s_refs[slot][h, early, early], -jnp.inf)
            s_el = s_refs[slot][h, early, late]
            s_ll = jnp.where(causal, s_refs[slot][h, late, late], -jnp.inf)
            m_e = jnp.max(s_ee, axis=0, keepdims=True)
            m_l = jnp.maximum(jnp.max(s_el, axis=0, keepdims=True), jnp.max(s_ll, axis=0, keepdims=True))
            acc_ref[i, h, :, early] = _dot(vt[:, early], jnp.exp2(s_ee - m_e).astype(BF16))
            acc_ref[i, h, :, late] = (_dot(vt[:, early], jnp.exp2(s_el - m_l).astype(BF16))
                                      + _dot(vt[:, late], jnp.exp2(s_ll - m_l).astype(BF16)))
            m_ref[i, h] = jnp.concatenate([m_e, m_l], axis=1)

    scores_diagonal(0, 0)
    for i in range(nblk):
        if i + 1 < nblk:
            scores_diagonal((i + 1) % 2, i + 1)
        attend_diagonal(i % 2, i)

    def following(task):
        i, t = task
        last = t + 1 == i
        return jnp.where(last, jnp.minimum(i + 1, nblk - 1), i), jnp.where(last, 0, t + 1)

    def plain_block(_, task):
        for u in range(ATT_TASKS_PER_BLOCK):
            nxt = following(task)
            scores(1 - u % 2, nxt)
            attend(u % 2, task)
            task = nxt
        return task

    n_plain = nblk * (nblk - 1) // 2
    assert ATT_TASKS_PER_BLOCK % 2 == 0 and n_plain % ATT_TASKS_PER_BLOCK == 0
    first = (jnp.int32(1), jnp.int32(0))
    scores(0, first)
    lax.fori_loop(0, n_plain // ATT_TASKS_PER_BLOCK, plain_block, first)

    lam_v = lam_ref[...]
    lam = (jnp.exp(jnp.sum(lam_v[0:1] * lam_v[1:2], axis=-1, keepdims=True))
           - jnp.exp(jnp.sum(lam_v[2:3] * lam_v[3:4], axis=-1, keepdims=True)) + LAMBDA_INIT)

    def finish_tile(i, carry):
        num = [acc_ref[i, h, 0:LANES, :] for h in heads]
        den = [acc_ref[i, h, LANES:LANES + 1, :] for h in heads]
        o = (num[0] / den[0] - lam * (num[1] / den[1])).T
        o = o * lax.rsqrt(jnp.mean(o * o, axis=-1, keepdims=True) + NORM_EPS)
        o_ref[0, pl.ds(pl.multiple_of(i * tq, tq), tq), :] = (o * nw_ref[...] * (1.0 - LAMBDA_INIT)).astype(BF16)
        return carry

    lax.fori_loop(0, nblk, finish_tile, 0, unroll=True)


def _diff_attn(dqt, dk, dvt, lam_vecs, nw):
    b, nblk, _, tq = dqt.shape
    s = nblk * tq
    return pl.pallas_call(
        _diff_attn_body,
        grid=(b, DIFF_HEADS),
        in_specs=[
            pl.BlockSpec((1, nblk, LANES, tq), lambda bi, j: (bi, 0, j, 0)),
            pl.BlockSpec((1, s, LANES), lambda bi, j: (bi, 0, j)),
            pl.BlockSpec((1, nblk, LANES, tq), lambda bi, j: (bi, 0, j, 0)),
            _resident((4, HEAD_DIM)),
            pl.BlockSpec((1, LANES), lambda bi, j: (0, j)),
        ],
        out_specs=pl.BlockSpec((1, s, LANES), lambda bi, j: (bi, 0, j)),
        out_shape=jax.ShapeDtypeStruct((b, s, DIFF_WIDTH), BF16),
        scratch_shapes=[pltpu.VMEM((2, tq, tq), F32)] * 2
                       + [pltpu.VMEM((2, 1, tq), F32)] * 2
                       + [pltpu.VMEM((nblk, 2, 1, tq), F32),
                          pltpu.VMEM((nblk, 2, LANES + ONES_ROWS, tq), F32)],
        compiler_params=pltpu.CompilerParams(
            dimension_semantics=("arbitrary", "arbitrary"), vmem_limit_bytes=VMEM_LIMIT),
        name="diff_attn",
    )(dqt, dk, dvt, lam_vecs, nw)


def _rotation_inv_freqs():
    half = HEAD_DIM // 2
    f32 = np.float32
    inv_ret = f32(1.0) / np.power(f32(RET_THETA), np.linspace(0.0, 1.0, half, dtype=f32))
    inv_rope = f32(1.0) / np.power(f32(ROPE_THETA), np.arange(0, HEAD_DIM, 2, dtype=f32) / f32(HEAD_DIM))
    return np.stack([inv_ret, inv_rope]).astype(f32)[:, :, None]


def _retention_tables():
    h, f32 = RET_HEADS, np.float32
    log_g = np.log(f32(1.0) - np.power(f32(2.0), f32(-5.0) - np.arange(h, dtype=f32))).astype(f32)
    idx = np.arange(CHUNK, dtype=f32)
    rel = idx[:, None] - idx[None, :]
    intra = np.where(rel >= 0, np.exp(log_g[:, None, None] * np.maximum(rel, f32(0.0))), f32(0.0)).astype(f32)
    q_decay = np.exp(log_g[:, None] * (idx + f32(1.0))).astype(f32)
    k_decay = np.exp(log_g[:, None] * (f32(CHUNK - 1.0) - idx)).astype(f32)
    chunk_decay = np.exp(log_g * f32(CHUNK)).astype(f32)

    value_head = np.arange(LANES) // HEAD_DIM
    qk_head = (np.arange(LANES) % HEAD_DIM) // (HEAD_DIM // 2)

    def slab(t, head_of_lane):
        return np.ascontiguousarray(t.reshape(h // 2, 2, CHUNK)[:, head_of_lane, :].transpose(0, 2, 1))

    cd = chunk_decay.reshape(h // 2, 2)[:, qk_head]
    cd = np.ascontiguousarray(np.broadcast_to(cd[:, :, None], (h // 2, LANES, LANES)))
    avg = (value_head[:, None] == value_head[None, :]).astype(f32) / f32(HEAD_DIM)
    return intra, slab(q_decay, value_head), slab(k_decay, qk_head), cd, jnp.asarray(np.concatenate([avg, avg]), BF16)


def _slab_perm(w, even_odd):
    k, half, slabs = w.shape[0], HEAD_DIM // 2, GROUP_WIDTH // LANES
    if even_odd:
        t = w.reshape(k, slabs, 2, half, 2).transpose(0, 1, 4, 2, 3)
    else:
        t = w.reshape(k, slabs, 2, 2, half).transpose(0, 1, 3, 2, 4)
    return t.reshape(k, GROUP_WIDTH)


def kernel(x, positions, ffn1_w_gate, ffn1_w_up, ffn1_w_down, ln1_w, ln1_b, w_in, ret_norm_w,
           diff_lambda_q1, diff_lambda_k1, diff_lambda_q2, diff_lambda_k2, diff_norm_w, w_out,
           ln2_w, ln2_b, ffn2_w_gate, ffn2_w_up, ffn2_w_down, ln3_w, ln3_b):
    b, s, d = x.shape
    n = b * s
    l = 0
    xf = x.reshape(n, d)
    pos = positions.reshape(b, 1, s)

    inv = _rotation_inv_freqs()
    intra, qd, kd, cd, avg = _retention_tables()
    lam_vecs = jnp.stack([diff_lambda_q1[l], diff_lambda_k1[l], diff_lambda_q2[l], diff_lambda_k2[l]])

    x1, w_in_bf16 = _ffn_ln(xf, ffn1_w_gate[l].astype(BF16), ffn1_w_up[l].astype(BF16), ffn1_w_down[l].astype(BF16),
                            ln1_w[l][None], ln1_b[l][None], cast=(w_in,))
    groups = [w_in_bf16[:, g * GROUP_WIDTH:(g + 1) * GROUP_WIDTH] for g in range(N_GROUPS)]
    for g, even_odd in ((0, True), (1, True), (4, False), (5, False)):
        groups[g] = _slab_perm(groups[g], even_odd)
    rq, rk, rv, rg, dqt, dk, dvt, wg2, wu2, wo = _in_proj(
        x1.reshape(b, s, d), pos, jnp.concatenate(groups, axis=1), inv, cast=(ffn2_w_gate, ffn2_w_up, w_out))
    ret, wd2 = _retention(rq, rk, rv, rg, ret_norm_w[l][None], intra, qd, kd, cd, avg, cast=(ffn2_w_down,))
    dif = _diff_attn(dqt, dk, dvt, lam_vecs, diff_norm_w[l][None])
    mixer = (ret.reshape(n, RET_WIDTH), dif.reshape(n, DIFF_WIDTH), wo, ln2_w[l][None], ln2_b[l][None])
    (out,) = _ffn_ln(x1, wg2, wu2, wd2, ln3_w[l][None], ln3_b[l][None], mixer=mixer)
    return out.reshape(b, s, d)
```

```python
import functools
import math

import numpy as np
import jax
import jax.numpy as jnp
from jax import lax
from jax.experimental import pallas as pl
from jax.experimental.pallas import tpu as pltpu

F32 = jnp.float32
BF16 = jnp.bfloat16

D_MODEL = 1024
D_FF = 2816
HEAD_DIM = 64
RET_HEADS = 8
RET_WIDTH = RET_HEADS * HEAD_DIM
DIFF_HEADS = 4
DIFF_VDIM = 2 * HEAD_DIM
DIFF_WIDTH = DIFF_HEADS * DIFF_VDIM
GROUP_WIDTH = 512
N_GROUPS = 7
CHUNK = 128
ROPE_THETA = 10000.0
RET_THETA = 10000.0
LN_EPS = 1e-5
NORM_EPS = 1e-6
DEPTH = 1
DEEPNORM_ALPHA = (2.0 * DEPTH) ** 0.25
LAMBDA_INIT = 0.8 - 0.6 * math.exp(-0.3 * 0)
LOG2_E = math.log2(math.e)

LANES = 128
VMEM_LIMIT = 56 * 1024 * 1024

TM_FFN = 1024
FFN_ROW_PARTS = 4
MXU_TILE = 256
FF_SPLIT = (D_FF // MXU_TILE // 2) * MXU_TILE
TM_PROJ = 512
TM_IN = 1024
TS_RET = 2048
ONES_ROWS = 16
ATT_TASKS_PER_BLOCK = 12


def _dot(a, b):
    return jnp.dot(a, b, preferred_element_type=F32)


def _dot_nt(a, b):
    return lax.dot_general(a, b, (((1,), (1,)), ((), ())), preferred_element_type=F32)


def _dot_tn(a, b):
    return lax.dot_general(a, b, (((0,), (0,)), ((), ())), preferred_element_type=F32)


def _layer_norm(z, w, b):
    mu = jnp.mean(z, axis=-1, keepdims=True)
    d = z - mu
    var = jnp.mean(d * d, axis=-1, keepdims=True)
    return d * lax.rsqrt(var + LN_EPS) * w + b


def _resident(shape):
    nd = len(shape)
    return pl.BlockSpec(shape, lambda *_: (0,) * nd, pipeline_mode=pl.Buffered(1))


def _cast_rows(src_refs, dst_refs):
    for src, dst in zip(src_refs, dst_refs):
        dst[...] = src[0].astype(BF16)


def _cast_specs(arrays, steps, step_of):
    assert all(a.shape[0] == 1 and a.shape[1] % (16 * steps) == 0 for a in arrays)
    in_specs = [pl.BlockSpec((1, a.shape[1] // steps, a.shape[2]), lambda *g: (0, step_of(*g), 0)) for a in arrays]
    out_specs = [pl.BlockSpec((a.shape[1] // steps, a.shape[2]), lambda *g: (step_of(*g), 0)) for a in arrays]
    return in_specs, out_specs, [jax.ShapeDtypeStruct(a.shape[1:], BF16) for a in arrays]


def _ffn_ln_body(*refs, with_mixer, n_cast):
    n_main = 11 if with_mixer else 6
    _cast_rows(refs[n_main:n_main + n_cast], refs[n_main + n_cast + 1:])
    o_ref = refs[n_main + n_cast]
    if with_mixer:
        x_ref, ret_ref, dif_ref, wo_ref, mlnw_ref, mlnb_ref, wg_ref, wu_ref, wd_ref, lnw_ref, lnb_ref = refs[:n_main]
    else:
        x_ref, wg_ref, wu_ref, wd_ref, lnw_ref, lnb_ref = refs[:n_main]
    part = x_ref.shape[0] // FFN_ROW_PARTS
    parts = [slice(r * part, (r + 1) * part) for r in range(FFN_ROW_PARTS)]

    def ffn_input(rows):
        x = x_ref[rows, :]
        if not with_mixer:
            return x
        mix = _dot(ret_ref[rows, :], wo_ref[0:RET_WIDTH, :]) + _dot(dif_ref[rows, :], wo_ref[RET_WIDTH:, :])
        return _layer_norm(DEEPNORM_ALPHA * x + mix, mlnw_ref[...], mlnb_ref[...])

    x_next = ffn_input(parts[0])
    for r, rows in enumerate(parts):
        x = x_next
        if r + 1 < len(parts):
            x_next = ffn_input(parts[r + 1])
        xb = x.astype(BF16)
        y = jnp.zeros(x.shape, F32)
        for sl in (slice(0, FF_SPLIT), slice(FF_SPLIT, D_FF)):
            g = _dot(xb, wg_ref[:, sl])
            u = _dot(xb, wu_ref[:, sl])
            h = (g * jax.nn.sigmoid(g) * u).astype(BF16)
            y = y + _dot(h, wd_ref[sl, :])
        z = DEEPNORM_ALPHA * x + 0.5 * y
        o_ref[rows, :] = _layer_norm(z, lnw_ref[...], lnb_ref[...])


def _ffn_ln(x, wg, wu, wd, lnw, lnb, mixer=None, cast=()):
    n = x.shape[0]
    tm = TM_FFN
    cast_in, cast_out, cast_shapes = _cast_specs(cast, n // tm, lambda i: i)
    rows = lambda width: pl.BlockSpec((tm, width), lambda i: (i, 0))
    vec = _resident((1, D_MODEL))
    operands, specs = [x], [rows(D_MODEL)]
    if mixer is not None:
        operands += list(mixer)
        specs += [rows(RET_WIDTH), rows(DIFF_WIDTH), _resident((RET_WIDTH + DIFF_WIDTH, D_MODEL)), vec, vec]
    operands += [wg, wu, wd, lnw, lnb]
    specs += [_resident((D_MODEL, D_FF)), _resident((D_MODEL, D_FF)), _resident((D_FF, D_MODEL)), vec, vec]
    return pl.pallas_call(
        functools.partial(_ffn_ln_body, with_mixer=mixer is not None, n_cast=len(cast)),
        grid=(n // tm,),
        in_specs=specs + cast_in,
        out_specs=[rows(D_MODEL)] + cast_out,
        out_shape=[jax.ShapeDtypeStruct((n, D_MODEL), F32)] + cast_shapes,
        compiler_params=pltpu.CompilerParams(
            dimension_semantics=("arbitrary",), vmem_limit_bytes=VMEM_LIMIT),
        name="mix_ffn_ln" if mixer is not None else "ffn_ln",
    )(*operands, *cast)


def _in_proj_body(x_ref, pos_ref, w_ref, inv_ref, *refs):
    n_cast = (len(refs) - N_GROUPS) // 2
    _cast_rows(refs[:n_cast], refs[n_cast + N_GROUPS:])
    rq_ref, rk_ref, rv_ref, rg_ref, dqt_ref, dk_ref, dvt_ref = refs[n_cast:n_cast + N_GROUPS]
    slabs = range(GROUP_WIDTH // LANES)
    for part in range(x_ref.shape[1] // TM_PROJ):
        rows = slice(part * TM_PROJ, (part + 1) * TM_PROJ)
        xb = x_ref[0, rows, :].astype(BF16)
        pos = pos_ref[0, :, rows].astype(F32)

        def tables(inv):
            ang = inv * pos
            c, s = jnp.cos(ang), jnp.sin(ang)
            return jnp.concatenate([c, c, c, c], axis=0).T, jnp.concatenate([-s, -s, s, s], axis=0).T

        def rotate(h, cs, scale):
            c, s = cs
            outs = []
            for j in slabs:
                hj = h[:, j * LANES:(j + 1) * LANES]
                r = hj * c + pltpu.roll(hj, LANES // 2, 1) * s
                outs.append(r if scale == 1.0 else r * scale)
            return outs

        def group(g):
            return _dot(xb, w_ref[:, g * GROUP_WIDTH:(g + 1) * GROUP_WIDTH])

        def store(ref, outs):
            ref[0, rows, :] = jnp.concatenate([o.astype(BF16) for o in outs], axis=1)

        def store_transposed(ref, outs):
            for j, o in enumerate(outs):
                ref[0, part, j * LANES:(j + 1) * LANES, :] = o.T.astype(BF16)

        h0 = group(0)
        ret_cs = tables(inv_ref[0])
        h1 = group(1)
        store(rq_ref, rotate(h0, ret_cs, 1.0))
        h2 = group(2)
        store(rk_ref, rotate(h1, ret_cs, HEAD_DIM ** -0.5))
        h3 = group(3)
        rv_ref[0, rows, :] = h2.astype(BF16)
        h4 = group(4)
        rg_ref[0, rows, :] = h3.astype(BF16)
        rope_cs = tables(inv_ref[1])
        h5 = group(5)
        store_transposed(dqt_ref, rotate(h4, rope_cs, HEAD_DIM ** -0.5 * LOG2_E))
        h6 = group(6)
        store(dk_ref, rotate(h5, rope_cs, 1.0))
        store_transposed(dvt_ref, [h6[:, j * LANES:(j + 1) * LANES] for j in slabs])


def _in_proj(x1, pos, w_in, inv, cast=()):
    b, s, _ = x1.shape
    tm, tq = TM_IN, TM_PROJ
    cast_in, cast_out, cast_shapes = _cast_specs(cast, b * (s // tm), lambda bi, i: bi * (s // tm) + i)
    row_spec = pl.BlockSpec((1, tm, GROUP_WIDTH), lambda bi, i: (bi, i, 0))
    col_spec = pl.BlockSpec((1, tm // tq, GROUP_WIDTH, tq), lambda bi, i: (bi, i, 0, 0))
    row_shape = jax.ShapeDtypeStruct((b, s, GROUP_WIDTH), BF16)
    col_shape = jax.ShapeDtypeStruct((b, s // tq, GROUP_WIDTH, tq), BF16)
    return pl.pallas_call(
        _in_proj_body,
        grid=(b, s // tm),
        in_specs=[
            pl.BlockSpec((1, tm, D_MODEL), lambda bi, i: (bi, i, 0)),
            pl.BlockSpec((1, 1, tm), lambda bi, i: (bi, 0, i)),
            _resident((D_MODEL, N_GROUPS * GROUP_WIDTH)),
            _resident((2, HEAD_DIM // 2, 1)),
        ] + cast_in,
        out_specs=[row_spec, row_spec, row_spec, row_spec, col_spec, row_spec, col_spec] + cast_out,
        out_shape=[row_shape, row_shape, row_shape, row_shape, col_shape, row_shape, col_shape] + cast_shapes,
        compiler_params=pltpu.CompilerParams(
            dimension_semantics=("arbitrary", "arbitrary"), vmem_limit_bytes=VMEM_LIMIT),
        name="in_proj",
    )(x1, pos, w_in, inv, *cast)


def _retention_body(q_ref, k_ref, v_ref, g_ref, nw_ref, intra_ref, qd_ref, kd_ref, cd_ref, avg_ref, *refs):
    n_cast = (len(refs) - 3) // 2
    o_ref, state_ref, raw_ref = refs[n_cast], refs[-2], refs[-1]
    _cast_rows(refs[:n_cast], refs[n_cast + 1:-2])
    @pl.when(pl.program_id(1) == 0)
    def _():
        state_ref[...] = jnp.zeros(state_ref.shape, F32)

    lane = lax.broadcasted_iota(jnp.int32, (CHUNK, LANES), 1)
    row = lax.broadcasted_iota(jnp.int32, (CHUNK, LANES), 0)
    out_a = lane < HEAD_DIM
    head_a = (lane % HEAD_DIM) < (HEAD_DIM // 2)
    same_head = ((row % HEAD_DIM) < (HEAD_DIM // 2)) == out_a
    avg = avg_ref[...]
    pairs = [slice(p * LANES, (p + 1) * LANES) for p in range(RET_HEADS // 2)]

    states = [state_ref[p] for p in range(len(pairs))]
    for c in range(q_ref.shape[1] // CHUNK):
        rows = slice(c * CHUNK, (c + 1) * CHUNK)
        qs = [q_ref[0, rows, cols] for cols in pairs]
        ks = [k_ref[0, rows, cols] for cols in pairs]
        vs = [v_ref[0, rows, cols] for cols in pairs]
        zero = jnp.zeros_like(qs[0])
        sa = [_dot_nt(jnp.where(head_a, q, zero), k) for q, k in zip(qs, ks)]
        sb = [_dot_nt(jnp.where(head_a, zero, q), k) for q, k in zip(qs, ks)]
        cross = [_dot(q, s.astype(BF16)) for q, s in zip(qs, states)]
        kv = [_dot_tn((k.astype(F32) * kd_ref[p]).astype(BF16), v) for p, (k, v) in enumerate(zip(ks, vs))]
        pa = [(s * intra_ref[2 * p]).astype(BF16) for p, s in enumerate(sa)]
        pb = [(s * intra_ref[2 * p + 1]).astype(BF16) for p, s in enumerate(sb)]
        ia = [_dot(x, v) for x, v in zip(pa, vs)]
        ib = [_dot(x, v) for x, v in zip(pb, vs)]
        for p, cols in enumerate(pairs):
            states[p] = states[p] * cd_ref[p] + jnp.where(same_head, kv[p], 0.0)
            raw_ref[rows, cols] = jnp.where(out_a, ia[p], ib[p]) + cross[p] * qd_ref[p]
    for p in range(len(pairs)):
        state_ref[p] = states[p]

    def head_mean(t):
        hi = t.astype(BF16)
        lo = (t - hi.astype(F32)).astype(BF16)
        return _dot(jnp.concatenate([hi, lo], axis=1), avg)

    for cols in pairs:
        o = raw_ref[:, cols]
        d = o - head_mean(o)
        y = d * lax.rsqrt(head_mean(d * d) + NORM_EPS)
        gate = g_ref[0, :, cols].astype(F32)
        o_ref[0, :, cols] = (gate * jax.nn.sigmoid(gate) * (y * nw_ref[:, cols])).astype(BF16)


def _retention(rq, rk, rv, rg, nw, intra, qd, kd, cd, avg, cast=()):
    b, s, _ = rq.shape
    ts = TS_RET
    cast_in, cast_out, cast_shapes = _cast_specs(cast, b * (s // ts), lambda bi, si: bi * (s // ts) + si)
    seq_spec = pl.BlockSpec((1, ts, RET_WIDTH), lambda bi, si: (bi, si, 0))
    return pl.pallas_call(
        _retention_body,
        grid=(b, s // ts),
        in_specs=[seq_spec, seq_spec, seq_spec, seq_spec,
                  _resident((1, RET_WIDTH)),
                  _resident(intra.shape), _resident(qd.shape), _resident(kd.shape),
                  _resident(cd.shape), _resident(avg.shape)] + cast_in,
        out_specs=[seq_spec] + cast_out,
        out_shape=[jax.ShapeDtypeStruct((b, s, RET_WIDTH), BF16)] + cast_shapes,
        scratch_shapes=[pltpu.VMEM((RET_HEADS // 2, LANES, LANES), F32),
                        pltpu.VMEM((ts, RET_WIDTH), F32)],
        compiler_params=pltpu.CompilerParams(
            dimension_semantics=("arbitrary", "arbitrary"), vmem_limit_bytes=VMEM_LIMIT),
        name="retention",
    )(rq, rk, rv, rg, nw, intra, qd, kd, cd, avg, *cast)


def _diff_attn_body(qt_ref, k_ref, vt_ref, lam_ref, nw_ref, o_ref, s0_ref, s1_ref, mx0_ref, mx1_ref, m_ref, acc_ref):
    nblk, _, tq = qt_ref.shape[1:]
    s_refs, mx_refs = (s0_ref, s1_ref), (mx0_ref, mx1_ref)
    row = lax.broadcasted_iota(jnp.int32, (LANES, 1), 0)
    first_head = (row % HEAD_DIM) < (HEAD_DIM // 2)
    heads = range(2)
    ones = jnp.ones((ONES_ROWS, tq), BF16)

    def scores(slot, task):
        i, t = task
        qt = qt_ref[0, i]
        zero = jnp.zeros_like(qt)
        k = k_ref[0, pl.ds(pl.multiple_of(t * tq, tq), tq), :]
        for h in heads:
            st = _dot(k, jnp.where(first_head, qt, zero) if h == 0 else jnp.where(first_head, zero, qt))
            s_refs[slot][h] = st
            mx_refs[slot][h] = jnp.max(st, axis=0, keepdims=True)

    def attend(slot, task):
        i, t = task
        vt = jnp.concatenate([vt_ref[0, t], ones], axis=0)
        for h in heads:
            m_old = m_ref[i, h]
            m_new = jnp.maximum(m_old, mx_refs[slot][h])
            p = jnp.exp2(s_refs[slot][h] - m_new)
            acc_ref[i, h] = jnp.exp2(m_old - m_new) * acc_ref[i, h] + _dot(vt, p.astype(BF16))
            m_ref[i, h] = m_new

    half = tq // 2
    early, late = slice(0, half), slice(half, tq)
    causal = (lax.broadcasted_iota(jnp.int32, (half, half), 0)
              <= lax.broadcasted_iota(jnp.int32, (half, half), 1))

    def scores_diagonal(slot, i):
        qt = qt_ref[0, i]
        zero = jnp.zeros_like(qt)
        k_early = k_ref[0, i * tq:i * tq + half, :]
        k_late = k_ref[0, i * tq + half:(i + 1) * tq, :]
        for h in heads:
            qh = jnp.where(first_head, qt, zero) if h == 0 else jnp.where(first_head, zero, qt)
            s_refs[slot][h, early, :] = _dot(k_early, qh)
            s_refs[slot][h, late, late] = _dot(k_late, qh[:, late])

    def attend_diagonal(slot, i):
        vt = jnp.concatenate([vt_ref[0, i], ones], axis=0)
        for h in heads:
            s_ee = jnp.where(causal, s_refs[slot][h, early, early], -jnp.inf)
            s_el = s_refs[slot][h, early, late]
            s_ll = jnp.where(causal, s_refs[slot][h, late, late], -jnp.inf)
            m_e = jnp.max(s_ee, axis=0, keepdims=True)
            m_l = jnp.maximum(jnp.max(s_el, axis=0, keepdims=True), jnp.max(s_ll, axis=0, keepdims=True))
            acc_ref[i, h, :, early] = _dot(vt[:, early], jnp.exp2(s_ee - m_e).astype(BF16))
            acc_ref[i, h, :, late] = (_dot(vt[:, early], jnp.exp2(s_el - m_l).astype(BF16))
                                      + _dot(vt[:, late], jnp.exp2(s_ll - m_l).astype(BF16)))
            m_ref[i, h] = jnp.concatenate([m_e, m_l], axis=1)

    scores_diagonal(0, 0)
    for i in range(nblk):
        if i + 1 < nblk:
            scores_diagonal((i + 1) % 2, i + 1)
        attend_diagonal(i % 2, i)

    def following(task):
        i, t = task
        last = t + 1 == i
        return jnp.where(last, jnp.minimum(i + 1, nblk - 1), i), jnp.where(last, 0, t + 1)

    def plain_block(_, task):
        for u in range(ATT_TASKS_PER_BLOCK):
            nxt = following(task)
            scores(1 - u % 2, nxt)
            attend(u % 2, task)
            task = nxt
        return task

    n_plain = nblk * (nblk - 1) // 2
    assert ATT_TASKS_PER_BLOCK % 2 == 0 and n_plain % ATT_TASKS_PER_BLOCK == 0
    first = (jnp.int32(1), jnp.int32(0))
    scores(0, first)
    lax.fori_loop(0, n_plain // ATT_TASKS_PER_BLOCK, plain_block, first)

    lam_v = lam_ref[...]
    lam = (jnp.exp(jnp.sum(lam_v[0:1] * lam_v[1:2], axis=-1, keepdims=True))
           - jnp.exp(jnp.sum(lam_v[2:3] * lam_v[3:4], axis=-1, keepdims=True)) + LAMBDA_INIT)

    def finish_tile(i, carry):
        num = [acc_ref[i, h, 0:LANES, :] for h in heads]
        den = [acc_ref[i, h, LANES:LANES + 1, :] for h in heads]
        o = num[0] / den[0] - lam * (num[1] / den[1])
        o = o * lax.rsqrt(jnp.mean(o * o, axis=0, keepdims=True) + NORM_EPS)
        o = (o * (nw_ref[...] * (1.0 - LAMBDA_INIT))).astype(BF16)
        o_ref[0, pl.ds(pl.multiple_of(i * tq, tq), tq), :] = o.T
        return carry

    lax.fori_loop(0, nblk, finish_tile, 0, unroll=True)


def _diff_attn(dqt, dk, dvt, lam_vecs, nw):
    b, nblk, _, tq = dqt.shape
    s = nblk * tq
    return pl.pallas_call(
        _diff_attn_body,
        grid=(b, DIFF_HEADS),
        in_specs=[
            pl.BlockSpec((1, nblk, LANES, tq), lambda bi, j: (bi, 0, j, 0)),
            pl.BlockSpec((1, s, LANES), lambda bi, j: (bi, 0, j)),
            pl.BlockSpec((1, nblk, LANES, tq), lambda bi, j: (bi, 0, j, 0)),
            _resident((4, HEAD_DIM)),
            pl.BlockSpec((LANES, 1), lambda bi, j: (j, 0)),
        ],
        out_specs=pl.BlockSpec((1, s, LANES), lambda bi, j: (bi, 0, j)),
        out_shape=jax.ShapeDtypeStruct((b, s, DIFF_WIDTH), BF16),
        scratch_shapes=[pltpu.VMEM((2, tq, tq), F32)] * 2
                       + [pltpu.VMEM((2, 1, tq), F32)] * 2
                       + [pltpu.VMEM((nblk, 2, 1, tq), F32),
                          pltpu.VMEM((nblk, 2, LANES + ONES_ROWS, tq), F32)],
        compiler_params=pltpu.CompilerParams(
            dimension_semantics=("arbitrary", "arbitrary"), vmem_limit_bytes=VMEM_LIMIT),
        name="diff_attn",
    )(dqt, dk, dvt, lam_vecs, nw)


def _rotation_inv_freqs():
    half = HEAD_DIM // 2
    f32 = np.float32
    inv_ret = f32(1.0) / np.power(f32(RET_THETA), np.linspace(0.0, 1.0, half, dtype=f32))
    inv_rope = f32(1.0) / np.power(f32(ROPE_THETA), np.arange(0, HEAD_DIM, 2, dtype=f32) / f32(HEAD_DIM))
    return np.stack([inv_ret, inv_rope]).astype(f32)[:, :, None]


def _retention_tables():
    h, f32 = RET_HEADS, np.float32
    log_g = np.log(f32(1.0) - np.power(f32(2.0), f32(-5.0) - np.arange(h, dtype=f32))).astype(f32)
    idx = np.arange(CHUNK, dtype=f32)
    rel = idx[:, None] - idx[None, :]
    intra = np.where(rel >= 0, np.exp(log_g[:, None, None] * np.maximum(rel, f32(0.0))), f32(0.0)).astype(f32)
    q_decay = np.exp(log_g[:, None] * (idx + f32(1.0))).astype(f32)
    k_decay = np.exp(log_g[:, None] * (f32(CHUNK - 1.0) - idx)).astype(f32)
    chunk_decay = np.exp(log_g * f32(CHUNK)).astype(f32)

    value_head = np.arange(LANES) // HEAD_DIM
    qk_head = (np.arange(LANES) % HEAD_DIM) // (HEAD_DIM // 2)

    def slab(t, head_of_lane):
        return np.ascontiguousarray(t.reshape(h // 2, 2, CHUNK)[:, head_of_lane, :].transpose(0, 2, 1))

    cd = chunk_decay.reshape(h // 2, 2)[:, qk_head]
    cd = np.ascontiguousarray(np.broadcast_to(cd[:, :, None], (h // 2, LANES, LANES)))
    avg = (value_head[:, None] == value_head[None, :]).astype(f32) / f32(HEAD_DIM)
    return intra, slab(q_decay, value_head), slab(k_decay, qk_head), cd, jnp.asarray(np.concatenate([avg, avg]), BF16)


def _slab_perm(w, even_odd):
    k, half, slabs = w.shape[0], HEAD_DIM // 2, GROUP_WIDTH // LANES
    if even_odd:
        t = w.reshape(k, slabs, 2, half, 2).transpose(0, 1, 4, 2, 3)
    else:
        t = w.reshape(k, slabs, 2, 2, half).transpose(0, 1, 3, 2, 4)
    return t.reshape(k, GROUP_WIDTH)


def kernel(x, positions, ffn1_w_gate, ffn1_w_up, ffn1_w_down, ln1_w, ln1_b, w_in, ret_norm_w,
           diff_lambda_q1, diff_lambda_k1, diff_lambda_q2, diff_lambda_k2, diff_norm_w, w_out,
           ln2_w, ln2_b, ffn2_w_gate, ffn2_w_up, ffn2_w_down, ln3_w, ln3_b):
    b, s, d = x.shape
    n = b * s
    l = 0
    xf = x.reshape(n, d)
    pos = positions.reshape(b, 1, s)

    inv = _rotation_inv_freqs()
    intra, qd, kd, cd, avg = _retention_tables()
    lam_vecs = jnp.stack([diff_lambda_q1[l], diff_lambda_k1[l], diff_lambda_q2[l], diff_lambda_k2[l]])

    x1, w_in_bf16 = _ffn_ln(xf, ffn1_w_gate[l].astype(BF16), ffn1_w_up[l].astype(BF16), ffn1_w_down[l].astype(BF16),
                            ln1_w[l][None], ln1_b[l][None], cast=(w_in,))
    groups = [w_in_bf16[:, g * GROUP_WIDTH:(g + 1) * GROUP_WIDTH] for g in range(N_GROUPS)]
    for g, even_odd in ((0, True), (1, True), (4, False), (5, False)):
        groups[g] = _slab_perm(groups[g], even_odd)
    rq, rk, rv, rg, dqt, dk, dvt, wg2, wu2, wo = _in_proj(
        x1.reshape(b, s, d), pos, jnp.concatenate(groups, axis=1), inv, cast=(ffn2_w_gate, ffn2_w_up, w_out))
    ret, wd2 = _retention(rq, rk, rv, rg, ret_norm_w[l][None], intra, qd, kd, cd, avg, cast=(ffn2_w_down,))
    dif = _diff_attn(dqt, dk, dvt, lam_vecs, diff_norm_w[l][:, None])
    mixer = (ret.reshape(n, RET_WIDTH), dif.reshape(n, DIFF_WIDTH), wo, ln2_w[l][None], ln2_b[l][None])
    (out,) = _ffn_ln(x1, wg2, wu2, wd2, ln3_w[l][None], ln3_b[l][None], mixer=mixer)
    return out.reshape(b, s, d)
```

```python
import functools
import math

import numpy as np
import jax
import jax.numpy as jnp
from jax import lax
from jax.experimental import pallas as pl
from jax.experimental.pallas import tpu as pltpu

F32 = jnp.float32
BF16 = jnp.bfloat16

D_MODEL = 1024
D_FF = 2816
HEAD_DIM = 64
RET_HEADS = 8
RET_WIDTH = RET_HEADS * HEAD_DIM
DIFF_HEADS = 4
DIFF_VDIM = 2 * HEAD_DIM
DIFF_WIDTH = DIFF_HEADS * DIFF_VDIM
GROUP_WIDTH = 512
N_GROUPS = 7
CHUNK = 128
ROPE_THETA = 10000.0
RET_THETA = 10000.0
LN_EPS = 1e-5
NORM_EPS = 1e-6
DEPTH = 1
DEEPNORM_ALPHA = (2.0 * DEPTH) ** 0.25
LAMBDA_INIT = 0.8 - 0.6 * math.exp(-0.3 * 0)
LOG2_E = math.log2(math.e)

LANES = 128
VMEM_LIMIT = 56 * 1024 * 1024

TM_FFN = 1024
FFN_ROW_PARTS = 4
MXU_TILE = 256
FF_SPLIT = (D_FF // MXU_TILE // 2) * MXU_TILE
TM_PROJ = 512
TM_IN = 1024
TS_RET = 2048
ONES_ROWS = 16
ATT_TASKS_PER_BLOCK = 24


def _dot(a, b):
    return jnp.dot(a, b, preferred_element_type=F32)


def _dot_nt(a, b):
    return lax.dot_general(a, b, (((1,), (1,)), ((), ())), preferred_element_type=F32)


def _dot_tn(a, b):
    return lax.dot_general(a, b, (((0,), (0,)), ((), ())), preferred_element_type=F32)


def _layer_norm(z, w, b):
    mu = jnp.mean(z, axis=-1, keepdims=True)
    d = z - mu
    var = jnp.mean(d * d, axis=-1, keepdims=True)
    return d * lax.rsqrt(var + LN_EPS) * w + b


def _resident(shape):
    nd = len(shape)
    return pl.BlockSpec(shape, lambda *_: (0,) * nd, pipeline_mode=pl.Buffered(1))


def _cast_rows(src_refs, dst_refs):
    for src, dst in zip(src_refs, dst_refs):
        dst[...] = src[0].astype(BF16)


def _cast_specs(arrays, steps, step_of):
    assert all(a.shape[0] == 1 and a.shape[1] % (16 * steps) == 0 for a in arrays)
    in_specs = [pl.BlockSpec((1, a.shape[1] // steps, a.shape[2]), lambda *g: (0, step_of(*g), 0)) for a in arrays]
    out_specs = [pl.BlockSpec((a.shape[1] // steps, a.shape[2]), lambda *g: (step_of(*g), 0)) for a in arrays]
    return in_specs, out_specs, [jax.ShapeDtypeStruct(a.shape[1:], BF16) for a in arrays]


def _ffn_ln_body(*refs, with_mixer, n_cast):
    n_main = 11 if with_mixer else 6
    _cast_rows(refs[n_main:n_main + n_cast], refs[n_main + n_cast + 1:])
    o_ref = refs[n_main + n_cast]
    if with_mixer:
        x_ref, ret_ref, dif_ref, wo_ref, mlnw_ref, mlnb_ref, wg_ref, wu_ref, wd_ref, lnw_ref, lnb_ref = refs[:n_main]
    else:
        x_ref, wg_ref, wu_ref, wd_ref, lnw_ref, lnb_ref = refs[:n_main]
    part = x_ref.shape[0] // FFN_ROW_PARTS
    parts = [slice(r * part, (r + 1) * part) for r in range(FFN_ROW_PARTS)]

    def ffn_input(rows):
        x = x_ref[rows, :]
        if not with_mixer:
            return x
        mix = _dot(ret_ref[rows, :], wo_ref[0:RET_WIDTH, :]) + _dot(dif_ref[rows, :], wo_ref[RET_WIDTH:, :])
        return _layer_norm(DEEPNORM_ALPHA * x + mix, mlnw_ref[...], mlnb_ref[...])

    x_next = ffn_input(parts[0])
    for r, rows in enumerate(parts):
        x = x_next
        if r + 1 < len(parts):
            x_next = ffn_input(parts[r + 1])
        xb = x.astype(BF16)
        y = jnp.zeros(x.shape, F32)
        for sl in (slice(0, FF_SPLIT), slice(FF_SPLIT, D_FF)):
            g = _dot(xb, wg_ref[:, sl])
            u = _dot(xb, wu_ref[:, sl])
            h = (g * jax.nn.sigmoid(g) * u).astype(BF16)
            y = y + _dot(h, wd_ref[sl, :])
        z = DEEPNORM_ALPHA * x + 0.5 * y
        o_ref[rows, :] = _layer_norm(z, lnw_ref[...], lnb_ref[...])


def _ffn_ln(x, wg, wu, wd, lnw, lnb, mixer=None, cast=()):
    n = x.shape[0]
    tm = TM_FFN
    cast_in, cast_out, cast_shapes = _cast_specs(cast, n // tm, lambda i: i)
    rows = lambda width: pl.BlockSpec((tm, width), lambda i: (i, 0))
    vec = _resident((1, D_MODEL))
    operands, specs = [x], [rows(D_MODEL)]
    if mixer is not None:
        operands += list(mixer)
        specs += [rows(RET_WIDTH), rows(DIFF_WIDTH), _resident((RET_WIDTH + DIFF_WIDTH, D_MODEL)), vec, vec]
    operands += [wg, wu, wd, lnw, lnb]
    specs += [_resident((D_MODEL, D_FF)), _resident((D_MODEL, D_FF)), _resident((D_FF, D_MODEL)), vec, vec]
    return pl.pallas_call(
        functools.partial(_ffn_ln_body, with_mixer=mixer is not None, n_cast=len(cast)),
        grid=(n // tm,),
        in_specs=specs + cast_in,
        out_specs=[rows(D_MODEL)] + cast_out,
        out_shape=[jax.ShapeDtypeStruct((n, D_MODEL), F32)] + cast_shapes,
        compiler_params=pltpu.CompilerParams(
            dimension_semantics=("arbitrary",), vmem_limit_bytes=VMEM_LIMIT),
        name="mix_ffn_ln" if mixer is not None else "ffn_ln",
    )(*operands, *cast)


def _in_proj_body(x_ref, pos_ref, w_ref, wrq_ref, wrk_ref, wdq_ref, wdk_ref, inv_ref, *refs):
    rotated = {0: wrq_ref, 1: wrk_ref, 4: wdq_ref, 5: wdk_ref}
    n_cast = (len(refs) - N_GROUPS) // 2
    _cast_rows(refs[:n_cast], refs[n_cast + N_GROUPS:])
    rq_ref, rk_ref, rv_ref, rg_ref, dqt_ref, dk_ref, dvt_ref = refs[n_cast:n_cast + N_GROUPS]
    slabs = range(GROUP_WIDTH // LANES)
    for part in range(x_ref.shape[1] // TM_PROJ):
        rows = slice(part * TM_PROJ, (part + 1) * TM_PROJ)
        xb = x_ref[0, rows, :].astype(BF16)
        pos = pos_ref[0, :, rows].astype(F32)

        def tables(inv):
            ang = inv * pos
            c, s = jnp.cos(ang), jnp.sin(ang)
            return jnp.concatenate([c, c, c, c], axis=0).T, jnp.concatenate([-s, -s, s, s], axis=0).T

        def rotate(h, cs, scale):
            c, s = cs
            outs = []
            for j in slabs:
                hj = h[:, j * LANES:(j + 1) * LANES]
                r = hj * c + pltpu.roll(hj, LANES // 2, 1) * s
                outs.append(r if scale == 1.0 else r * scale)
            return outs

        def group(g):
            return _dot(xb, rotated[g][...] if g in rotated else w_ref[:, g * GROUP_WIDTH:(g + 1) * GROUP_WIDTH])

        def store(ref, outs):
            ref[0, rows, :] = jnp.concatenate([o.astype(BF16) for o in outs], axis=1)

        def store_transposed(ref, outs):
            for j, o in enumerate(outs):
                ref[0, part, j * LANES:(j + 1) * LANES, :] = o.T.astype(BF16)

        h0 = group(0)
        ret_cs = tables(inv_ref[0])
        h1 = group(1)
        store(rq_ref, rotate(h0, ret_cs, 1.0))
        h2 = group(2)
        store(rk_ref, rotate(h1, ret_cs, HEAD_DIM ** -0.5))
        h3 = group(3)
        rv_ref[0, rows, :] = h2.astype(BF16)
        h4 = group(4)
        rg_ref[0, rows, :] = h3.astype(BF16)
        rope_cs = tables(inv_ref[1])
        h5 = group(5)
        store_transposed(dqt_ref, rotate(h4, rope_cs, HEAD_DIM ** -0.5 * LOG2_E))
        h6 = group(6)
        store(dk_ref, rotate(h5, rope_cs, 1.0))
        store_transposed(dvt_ref, [h6[:, j * LANES:(j + 1) * LANES] for j in slabs])


def _in_proj(x1, pos, w_in, w_rotated, inv, cast=()):
    b, s, _ = x1.shape
    tm, tq = TM_IN, TM_PROJ
    cast_in, cast_out, cast_shapes = _cast_specs(cast, b * (s // tm), lambda bi, i: bi * (s // tm) + i)
    row_spec = pl.BlockSpec((1, tm, GROUP_WIDTH), lambda bi, i: (bi, i, 0))
    col_spec = pl.BlockSpec((1, tm // tq, GROUP_WIDTH, tq), lambda bi, i: (bi, i, 0, 0))
    row_shape = jax.ShapeDtypeStruct((b, s, GROUP_WIDTH), BF16)
    col_shape = jax.ShapeDtypeStruct((b, s // tq, GROUP_WIDTH, tq), BF16)
    return pl.pallas_call(
        _in_proj_body,
        grid=(b, s // tm),
        in_specs=[
            pl.BlockSpec((1, tm, D_MODEL), lambda bi, i: (bi, i, 0)),
            pl.BlockSpec((1, 1, tm), lambda bi, i: (bi, 0, i)),
            _resident((D_MODEL, N_GROUPS * GROUP_WIDTH)),
            *[_resident((D_MODEL, GROUP_WIDTH))] * len(w_rotated),
            _resident((2, HEAD_DIM // 2, 1)),
        ] + cast_in,
        out_specs=[row_spec, row_spec, row_spec, row_spec, col_spec, row_spec, col_spec] + cast_out,
        out_shape=[row_shape, row_shape, row_shape, row_shape, col_shape, row_shape, col_shape] + cast_shapes,
        compiler_params=pltpu.CompilerParams(
            dimension_semantics=("arbitrary", "arbitrary"), vmem_limit_bytes=VMEM_LIMIT),
        name="in_proj",
    )(x1, pos, w_in, *w_rotated, inv, *cast)


def _retention_body(q_ref, k_ref, v_ref, g_ref, nw_ref, intra_ref, qd_ref, kd_ref, cd_ref, avg_ref, *refs):
    n_cast = (len(refs) - 3) // 2
    o_ref, state_ref, raw_ref = refs[n_cast], refs[-2], refs[-1]
    _cast_rows(refs[:n_cast], refs[n_cast + 1:-2])
    @pl.when(pl.program_id(1) == 0)
    def _():
        state_ref[...] = jnp.zeros(state_ref.shape, F32)

    lane = lax.broadcasted_iota(jnp.int32, (CHUNK, LANES), 1)
    row = lax.broadcasted_iota(jnp.int32, (CHUNK, LANES), 0)
    out_a = lane < HEAD_DIM
    head_a = (lane % HEAD_DIM) < (HEAD_DIM // 2)
    same_head = ((row % HEAD_DIM) < (HEAD_DIM // 2)) == out_a
    avg = avg_ref[...]
    pairs = [slice(p * LANES, (p + 1) * LANES) for p in range(RET_HEADS // 2)]

    states = [state_ref[p] for p in range(len(pairs))]
    for c in range(q_ref.shape[1] // CHUNK):
        rows = slice(c * CHUNK, (c + 1) * CHUNK)
        qs = [q_ref[0, rows, cols] for cols in pairs]
        ks = [k_ref[0, rows, cols] for cols in pairs]
        vs = [v_ref[0, rows, cols] for cols in pairs]
        zero = jnp.zeros_like(qs[0])
        sa = [_dot_nt(jnp.where(head_a, q, zero), k) for q, k in zip(qs, ks)]
        sb = [_dot_nt(jnp.where(head_a, zero, q), k) for q, k in zip(qs, ks)]
        cross = [_dot(q, s.astype(BF16)) for q, s in zip(qs, states)]
        kv = [_dot_tn((k.astype(F32) * kd_ref[p]).astype(BF16), v) for p, (k, v) in enumerate(zip(ks, vs))]
        pa = [(s * intra_ref[2 * p]).astype(BF16) for p, s in enumerate(sa)]
        pb = [(s * intra_ref[2 * p + 1]).astype(BF16) for p, s in enumerate(sb)]
        ia = [_dot(x, v) for x, v in zip(pa, vs)]
        ib = [_dot(x, v) for x, v in zip(pb, vs)]
        for p, cols in enumerate(pairs):
            states[p] = states[p] * cd_ref[p] + jnp.where(same_head, kv[p], 0.0)
            raw_ref[rows, cols] = jnp.where(out_a, ia[p], ib[p]) + cross[p] * qd_ref[p]
    for p in range(len(pairs)):
        state_ref[p] = states[p]

    def head_mean(t):
        hi = t.astype(BF16)
        lo = (t - hi.astype(F32)).astype(BF16)
        return _dot(jnp.concatenate([hi, lo], axis=1), avg)

    for cols in pairs:
        o = raw_ref[:, cols]
        d = o - head_mean(o)
        y = d * lax.rsqrt(head_mean(d * d) + NORM_EPS)
        gate = g_ref[0, :, cols].astype(F32)
        o_ref[0, :, cols] = (gate * jax.nn.sigmoid(gate) * (y * nw_ref[:, cols])).astype(BF16)


def _retention(rq, rk, rv, rg, nw, intra, qd, kd, cd, avg, cast=()):
    b, s, _ = rq.shape
    ts = TS_RET
    cast_in, cast_out, cast_shapes = _cast_specs(cast, b * (s // ts), lambda bi, si: bi * (s // ts) + si)
    seq_spec = pl.BlockSpec((1, ts, RET_WIDTH), lambda bi, si: (bi, si, 0))
    return pl.pallas_call(
        _retention_body,
        grid=(b, s // ts),
        in_specs=[seq_spec, seq_spec, seq_spec, seq_spec,
                  _resident((1, RET_WIDTH)),
                  _resident(intra.shape), _resident(qd.shape), _resident(kd.shape),
                  _resident(cd.shape), _resident(avg.shape)] + cast_in,
        out_specs=[seq_spec] + cast_out,
        out_shape=[jax.ShapeDtypeStruct((b, s, RET_WIDTH), BF16)] + cast_shapes,
        scratch_shapes=[pltpu.VMEM((RET_HEADS // 2, LANES, LANES), F32),
                        pltpu.VMEM((ts, RET_WIDTH), F32)],
        compiler_params=pltpu.CompilerParams(
            dimension_semantics=("arbitrary", "arbitrary"), vmem_limit_bytes=VMEM_LIMIT),
        name="retention",
    )(rq, rk, rv, rg, nw, intra, qd, kd, cd, avg, *cast)


def _diff_attn_body(qt_ref, k_ref, vt_ref, lq1_ref, lk1_ref, lq2_ref, lk2_ref, nw_ref, o_ref,
                    s0_ref, s1_ref, mx0_ref, mx1_ref, m_ref, acc_ref):
    nblk, _, tq = qt_ref.shape[1:]
    s_refs, mx_refs = (s0_ref, s1_ref), (mx0_ref, mx1_ref)
    row = lax.broadcasted_iota(jnp.int32, (LANES, 1), 0)
    first_head = (row % HEAD_DIM) < (HEAD_DIM // 2)
    heads = range(2)
    ones = jnp.ones((ONES_ROWS, tq), BF16)

    def scores(slot, task):
        i, t = task
        qt = qt_ref[0, i]
        zero = jnp.zeros_like(qt)
        k = k_ref[0, pl.ds(pl.multiple_of(t * tq, tq), tq), :]
        for h in heads:
            st = _dot(k, jnp.where(first_head, qt, zero) if h == 0 else jnp.where(first_head, zero, qt))
            s_refs[slot][h] = st
            mx_refs[slot][h] = jnp.max(st, axis=0, keepdims=True)

    def attend(slot, task):
        i, t = task
        vt = jnp.concatenate([vt_ref[0, t], ones], axis=0)
        for h in heads:
            m_old = m_ref[i, h]
            m_new = jnp.maximum(m_old, mx_refs[slot][h])
            p = jnp.exp2(s_refs[slot][h] - m_new)
            acc_ref[i, h] = jnp.exp2(m_old - m_new) * acc_ref[i, h] + _dot(vt, p.astype(BF16))
            m_ref[i, h] = m_new

    half = tq // 2
    early, late = slice(0, half), slice(half, tq)
    causal = (lax.broadcasted_iota(jnp.int32, (half, half), 0)
              <= lax.broadcasted_iota(jnp.int32, (half, half), 1))

    def scores_diagonal(slot, i):
        qt = qt_ref[0, i]
        zero = jnp.zeros_like(qt)
        k_early = k_ref[0, i * tq:i * tq + half, :]
        k_late = k_ref[0, i * tq + half:(i + 1) * tq, :]
        for h in heads:
            qh = jnp.where(first_head, qt, zero) if h == 0 else jnp.where(first_head, zero, qt)
            s_refs[slot][h, early, :] = _dot(k_early, qh)
            s_refs[slot][h, late, late] = _dot(k_late, qh[:, late])

    def attend_diagonal(slot, i):
        vt = jnp.concatenate([vt_ref[0, i], ones], axis=0)
        for h in heads:
            s_ee = jnp.where(causal, s_refs[slot][h, early, early], -jnp.inf)
            s_el = s_refs[slot][h, early, late]
            s_ll = jnp.where(causal, s_refs[slot][h, late, late], -jnp.inf)
            m_e = jnp.max(s_ee, axis=0, keepdims=True)
            m_l = jnp.maximum(jnp.max(s_el, axis=0, keepdims=True), jnp.max(s_ll, axis=0, keepdims=True))
            acc_ref[i, h, :, early] = _dot(vt[:, early], jnp.exp2(s_ee - m_e).astype(BF16))
            acc_ref[i, h, :, late] = (_dot(vt[:, early], jnp.exp2(s_el - m_l).astype(BF16))
                                      + _dot(vt[:, late], jnp.exp2(s_ll - m_l).astype(BF16)))
            m_ref[i, h] = jnp.concatenate([m_e, m_l], axis=1)

    scores_diagonal(0, 0)
    for i in range(nblk):
        if i + 1 < nblk:
            scores_diagonal((i + 1) % 2, i + 1)
        attend_diagonal(i % 2, i)

    def following(task):
        i, t = task
        last = t + 1 == i
        return jnp.where(last, jnp.minimum(i + 1, nblk - 1), i), jnp.where(last, 0, t + 1)

    def plain_block(_, task):
        for u in range(ATT_TASKS_PER_BLOCK):
            nxt = following(task)
            scores(1 - u % 2, nxt)
            attend(u % 2, task)
            task = nxt
        return task

    n_plain = nblk * (nblk - 1) // 2
    assert ATT_TASKS_PER_BLOCK % 2 == 0 and n_plain % ATT_TASKS_PER_BLOCK == 0
    first = (jnp.int32(1), jnp.int32(0))
    scores(0, first)
    lax.fori_loop(0, n_plain // ATT_TASKS_PER_BLOCK, plain_block, first)

    lam = (jnp.exp(jnp.sum(lq1_ref[...] * lk1_ref[...], axis=-1, keepdims=True))
           - jnp.exp(jnp.sum(lq2_ref[...] * lk2_ref[...], axis=-1, keepdims=True)) + LAMBDA_INIT)

    def finish_tile(i, carry):
        num = [acc_ref[i, h, 0:LANES, :] for h in heads]
        den = [acc_ref[i, h, LANES:LANES + 1, :] for h in heads]
        o = num[0] / den[0] - lam * (num[1] / den[1])
        o = o * lax.rsqrt(jnp.mean(o * o, axis=0, keepdims=True) + NORM_EPS)
        o = (o * (nw_ref[...] * (1.0 - LAMBDA_INIT))).astype(BF16)
        o_ref[0, pl.ds(pl.multiple_of(i * tq, tq), tq), :] = o.T
        return carry

    lax.fori_loop(0, nblk, finish_tile, 0, unroll=True)


def _diff_attn(dqt, dk, dvt, lam_vecs, nw):
    b, nblk, _, tq = dqt.shape
    s = nblk * tq
    return pl.pallas_call(
        _diff_attn_body,
        grid=(b, DIFF_HEADS),
        in_specs=[
            pl.BlockSpec((1, nblk, LANES, tq), lambda bi, j: (bi, 0, j, 0)),
            pl.BlockSpec((1, s, LANES), lambda bi, j: (bi, 0, j)),
            pl.BlockSpec((1, nblk, LANES, tq), lambda bi, j: (bi, 0, j, 0)),
            *[_resident((1, HEAD_DIM))] * len(lam_vecs),
            pl.BlockSpec((LANES, 1), lambda bi, j: (j, 0)),
        ],
        out_specs=pl.BlockSpec((1, s, LANES), lambda bi, j: (bi, 0, j)),
        out_shape=jax.ShapeDtypeStruct((b, s, DIFF_WIDTH), BF16),
        scratch_shapes=[pltpu.VMEM((2, tq, tq), F32)] * 2
                       + [pltpu.VMEM((2, 1, tq), F32)] * 2
                       + [pltpu.VMEM((nblk, 2, 1, tq), F32),
                          pltpu.VMEM((nblk, 2, LANES + ONES_ROWS, tq), F32)],
        compiler_params=pltpu.CompilerParams(
            dimension_semantics=("arbitrary", "arbitrary"), vmem_limit_bytes=VMEM_LIMIT),
        name="diff_attn",
    )(dqt, dk, dvt, *lam_vecs, nw)


def _rotation_inv_freqs():
    half = HEAD_DIM // 2
    f32 = np.float32
    inv_ret = f32(1.0) / np.power(f32(RET_THETA), np.linspace(0.0, 1.0, half, dtype=f32))
    inv_rope = f32(1.0) / np.power(f32(ROPE_THETA), np.arange(0, HEAD_DIM, 2, dtype=f32) / f32(HEAD_DIM))
    return np.stack([inv_ret, inv_rope]).astype(f32)[:, :, None]


def _retention_tables():
    h, f32 = RET_HEADS, np.float32
    log_g = np.log(f32(1.0) - np.power(f32(2.0), f32(-5.0) - np.arange(h, dtype=f32))).astype(f32)
    idx = np.arange(CHUNK, dtype=f32)
    rel = idx[:, None] - idx[None, :]
    intra = np.where(rel >= 0, np.exp(log_g[:, None, None] * np.maximum(rel, f32(0.0))), f32(0.0)).astype(f32)
    q_decay = np.exp(log_g[:, None] * (idx + f32(1.0))).astype(f32)
    k_decay = np.exp(log_g[:, None] * (f32(CHUNK - 1.0) - idx)).astype(f32)
    chunk_decay = np.exp(log_g * f32(CHUNK)).astype(f32)

    value_head = np.arange(LANES) // HEAD_DIM
    qk_head = (np.arange(LANES) % HEAD_DIM) // (HEAD_DIM // 2)

    def slab(t, head_of_lane):
        return np.ascontiguousarray(t.reshape(h // 2, 2, CHUNK)[:, head_of_lane, :].transpose(0, 2, 1))

    cd = chunk_decay.reshape(h // 2, 2)[:, qk_head]
    cd = np.ascontiguousarray(np.broadcast_to(cd[:, :, None], (h // 2, LANES, LANES)))
    avg = (value_head[:, None] == value_head[None, :]).astype(f32) / f32(HEAD_DIM)
    return intra, slab(q_decay, value_head), slab(k_decay, qk_head), cd, jnp.asarray(np.concatenate([avg, avg]), BF16)


def _slab_perm(w, even_odd):
    k, half, slabs = w.shape[0], HEAD_DIM // 2, GROUP_WIDTH // LANES
    if even_odd:
        t = w.reshape(k, slabs, 2, half, 2).transpose(0, 1, 4, 2, 3)
    else:
        t = w.reshape(k, slabs, 2, 2, half).transpose(0, 1, 3, 2, 4)
    return t.reshape(k, GROUP_WIDTH)


def kernel(x, positions, ffn1_w_gate, ffn1_w_up, ffn1_w_down, ln1_w, ln1_b, w_in, ret_norm_w,
           diff_lambda_q1, diff_lambda_k1, diff_lambda_q2, diff_lambda_k2, diff_norm_w, w_out,
           ln2_w, ln2_b, ffn2_w_gate, ffn2_w_up, ffn2_w_down, ln3_w, ln3_b):
    b, s, d = x.shape
    n = b * s
    l = 0
    xf = x.reshape(n, d)
    pos = positions.reshape(b, 1, s)

    inv = _rotation_inv_freqs()
    intra, qd, kd, cd, avg = _retention_tables()
    lam_vecs = (diff_lambda_q1, diff_lambda_k1, diff_lambda_q2, diff_lambda_k2)

    x1, w_in_bf16 = _ffn_ln(xf, ffn1_w_gate[l].astype(BF16), ffn1_w_up[l].astype(BF16), ffn1_w_down[l].astype(BF16),
                            ln1_w[l][None], ln1_b[l][None], cast=(w_in,))
    w_rotated = [_slab_perm(w_in_bf16[:, g * GROUP_WIDTH:(g + 1) * GROUP_WIDTH], even_odd)
                 for g, even_odd in ((0, True), (1, True), (4, False), (5, False))]
    rq, rk, rv, rg, dqt, dk, dvt, wg2, wu2, wo = _in_proj(
        x1.reshape(b, s, d), pos, w_in_bf16, w_rotated, inv, cast=(ffn2_w_gate, ffn2_w_up, w_out))
    ret, wd2 = _retention(rq, rk, rv, rg, ret_norm_w[l][None], intra, qd, kd, cd, avg, cast=(ffn2_w_down,))
    dif = _diff_attn(dqt, dk, dvt, lam_vecs, diff_norm_w[l][:, None])
    mixer = (ret.reshape(n, RET_WIDTH), dif.reshape(n, DIFF_WIDTH), wo, ln2_w[l][None], ln2_b[l][None])
    (out,) = _ffn_ln(x1, wg2, wu2, wd2, ln3_w[l][None], ln3_b[l][None], mixer=mixer)
    return out.reshape(b, s, d)
```

```python
import functools
import math

import numpy as np
import jax
import jax.numpy as jnp
from jax import lax
from jax.experimental import pallas as pl
from jax.experimental.pallas import tpu as pltpu

F32 = jnp.float32
BF16 = jnp.bfloat16

D_MODEL = 1024
D_FF = 2816
HEAD_DIM = 64
RET_HEADS = 8
RET_WIDTH = RET_HEADS * HEAD_DIM
DIFF_HEADS = 4
DIFF_VDIM = 2 * HEAD_DIM
DIFF_WIDTH = DIFF_HEADS * DIFF_VDIM
GROUP_WIDTH = 512
N_GROUPS = 7
CHUNK = 128
ROPE_THETA = 10000.0
RET_THETA = 10000.0
LN_EPS = 1e-5
NORM_EPS = 1e-6
DEPTH = 1
DEEPNORM_ALPHA = (2.0 * DEPTH) ** 0.25
LAMBDA_INIT = 0.8 - 0.6 * math.exp(-0.3 * 0)
LOG2_E = math.log2(math.e)

LANES = 128
VMEM_LIMIT = 56 * 1024 * 1024

TM_FFN = 1024
FFN_ROW_PARTS = 4
MXU_TILE = 256
FF_SPLIT = (D_FF // MXU_TILE // 2) * MXU_TILE
TM_PROJ = 512
TM_IN = 1024
TS_RET = 2048
BF16_SUBLANES = 16
ONES_ROWS = BF16_SUBLANES
ATT_TASKS_PER_BLOCK = 24


def _dot(a, b):
    return jnp.dot(a, b, preferred_element_type=F32)


def _dot_nt(a, b):
    return lax.dot_general(a, b, (((1,), (1,)), ((), ())), preferred_element_type=F32)


def _dot_tn(a, b):
    return lax.dot_general(a, b, (((0,), (0,)), ((), ())), preferred_element_type=F32)


def _layer_norm(z, w, b):
    mu = jnp.mean(z, axis=-1, keepdims=True)
    d = z - mu
    var = jnp.mean(d * d, axis=-1, keepdims=True)
    return d * lax.rsqrt(var + LN_EPS) * w + b


def _resident(shape):
    nd = len(shape)
    return pl.BlockSpec(shape, lambda *_: (0,) * nd, pipeline_mode=pl.Buffered(1))


def _cast_rows(src_refs, dst_refs):
    for src, dst in zip(src_refs, dst_refs):
        dst[...] = src[0].astype(BF16)


def _cast_specs(arrays, steps, step_of):
    assert all(a.shape[0] == 1 and a.shape[1] % (BF16_SUBLANES * steps) == 0 for a in arrays)
    in_specs = [pl.BlockSpec((1, a.shape[1] // steps, a.shape[2]), lambda *g: (0, step_of(*g), 0)) for a in arrays]
    out_specs = [pl.BlockSpec((a.shape[1] // steps, a.shape[2]), lambda *g: (step_of(*g), 0)) for a in arrays]
    return in_specs, out_specs, [jax.ShapeDtypeStruct(a.shape[1:], BF16) for a in arrays]


def _ffn_ln_body(*refs, with_mixer, n_cast):
    n_main = 11 if with_mixer else 6
    _cast_rows(refs[n_main:n_main + n_cast], refs[n_main + n_cast + 1:])
    o_ref = refs[n_main + n_cast]
    if with_mixer:
        x_ref, ret_ref, dif_ref, wo_ref, mlnw_ref, mlnb_ref, wg_ref, wu_ref, wd_ref, lnw_ref, lnb_ref = refs[:n_main]
    else:
        x_ref, wg_ref, wu_ref, wd_ref, lnw_ref, lnb_ref = refs[:n_main]
    part = x_ref.shape[0] // FFN_ROW_PARTS
    parts = [slice(r * part, (r + 1) * part) for r in range(FFN_ROW_PARTS)]

    def ffn_input(rows):
        x = x_ref[rows, :]
        if not with_mixer:
            return x
        mix = _dot(ret_ref[rows, :], wo_ref[0:RET_WIDTH, :]) + _dot(dif_ref[rows, :], wo_ref[RET_WIDTH:, :])
        return _layer_norm(DEEPNORM_ALPHA * x + mix, mlnw_ref[...], mlnb_ref[...])

    x_next = ffn_input(parts[0])
    for r, rows in enumerate(parts):
        x = x_next
        if r + 1 < len(parts):
            x_next = ffn_input(parts[r + 1])
        xb = x.astype(BF16)
        y = jnp.zeros(x.shape, F32)
        for sl in (slice(0, FF_SPLIT), slice(FF_SPLIT, D_FF)):
            g = _dot(xb, wg_ref[:, sl])
            u = _dot(xb, wu_ref[:, sl])
            h = (g * jax.nn.sigmoid(g) * u).astype(BF16)
            y = y + _dot(h, wd_ref[sl, :])
        z = DEEPNORM_ALPHA * x + 0.5 * y
        o_ref[rows, :] = _layer_norm(z, lnw_ref[...], lnb_ref[...])


def _ffn_ln(x, wg, wu, wd, lnw, lnb, mixer=None, cast=()):
    n = x.shape[0]
    tm = TM_FFN
    cast_in, cast_out, cast_shapes = _cast_specs(cast, n // tm, lambda i: i)
    rows = lambda width: pl.BlockSpec((tm, width), lambda i: (i, 0))
    vec = _resident((1, D_MODEL))
    operands, specs = [x], [rows(D_MODEL)]
    if mixer is not None:
        operands += list(mixer)
        specs += [rows(RET_WIDTH), rows(DIFF_WIDTH), _resident((RET_WIDTH + DIFF_WIDTH, D_MODEL)), vec, vec]
    operands += [wg, wu, wd, lnw, lnb]
    specs += [_resident((D_MODEL, D_FF)), _resident((D_MODEL, D_FF)), _resident((D_FF, D_MODEL)), vec, vec]
    return pl.pallas_call(
        functools.partial(_ffn_ln_body, with_mixer=mixer is not None, n_cast=len(cast)),
        grid=(n // tm,),
        in_specs=specs + cast_in,
        out_specs=[rows(D_MODEL)] + cast_out,
        out_shape=[jax.ShapeDtypeStruct((n, D_MODEL), F32)] + cast_shapes,
        compiler_params=pltpu.CompilerParams(
            dimension_semantics=("arbitrary",), vmem_limit_bytes=VMEM_LIMIT),
        name="mix_ffn_ln" if mixer is not None else "ffn_ln",
    )(*operands, *cast)


def _in_proj_body(x_ref, pos_ref, w_ref, wrq_ref, wrk_ref, wdq_ref, wdk_ref, inv_ref, *refs):
    rotated = {0: wrq_ref, 1: wrk_ref, 4: wdq_ref, 5: wdk_ref}
    n_cast = (len(refs) - N_GROUPS) // 2
    _cast_rows(refs[:n_cast], refs[n_cast + N_GROUPS:])
    rq_ref, rk_ref, rv_ref, rg_ref, dqt_ref, dk_ref, dvt_ref = refs[n_cast:n_cast + N_GROUPS]
    slabs = range(GROUP_WIDTH // LANES)
    for part in range(x_ref.shape[1] // TM_PROJ):
        rows = slice(part * TM_PROJ, (part + 1) * TM_PROJ)
        xb = x_ref[0, rows, :].astype(BF16)
        pos = pos_ref[0, :, rows].astype(F32)

        def tables(inv):
            ang = inv * pos
            c, s = jnp.cos(ang), jnp.sin(ang)
            return jnp.concatenate([c, c, c, c], axis=0).T, jnp.concatenate([-s, -s, s, s], axis=0).T

        def rotate(h, cs, scale):
            c, s = cs
            outs = []
            for j in slabs:
                hj = h[:, j * LANES:(j + 1) * LANES]
                r = hj * c + pltpu.roll(hj, LANES // 2, 1) * s
                outs.append(r if scale == 1.0 else r * scale)
            return outs

        def group(g):
            return _dot(xb, rotated[g][...] if g in rotated else w_ref[:, g * GROUP_WIDTH:(g + 1) * GROUP_WIDTH])

        def store(ref, outs):
            ref[0, rows, :] = jnp.concatenate([o.astype(BF16) for o in outs], axis=1)

        def store_transposed(ref, outs):
            for j, o in enumerate(outs):
                ref[0, part, j * LANES:(j + 1) * LANES, :] = o.T.astype(BF16)

        h0 = group(0)
        ret_cs = tables(inv_ref[0])
        h1 = group(1)
        store(rq_ref, rotate(h0, ret_cs, 1.0))
        h2 = group(2)
        store(rk_ref, rotate(h1, ret_cs, HEAD_DIM ** -0.5))
        h3 = group(3)
        rv_ref[0, rows, :] = h2.astype(BF16)
        h4 = group(4)
        rg_ref[0, rows, :] = h3.astype(BF16)
        rope_cs = tables(inv_ref[1])
        h5 = group(5)
        store_transposed(dqt_ref, rotate(h4, rope_cs, HEAD_DIM ** -0.5 * LOG2_E))
        h6 = group(6)
        store(dk_ref, rotate(h5, rope_cs, 1.0))
        store_transposed(dvt_ref, [h6[:, j * LANES:(j + 1) * LANES] for j in slabs])


def _in_proj(x1, pos, w_in, w_rotated, inv, cast=()):
    b, s, _ = x1.shape
    tm, tq = TM_IN, TM_PROJ
    cast_in, cast_out, cast_shapes = _cast_specs(cast, b * (s // tm), lambda bi, i: bi * (s // tm) + i)
    row_spec = pl.BlockSpec((1, tm, GROUP_WIDTH), lambda bi, i: (bi, i, 0))
    col_spec = pl.BlockSpec((1, tm // tq, GROUP_WIDTH, tq), lambda bi, i: (bi, i, 0, 0))
    row_shape = jax.ShapeDtypeStruct((b, s, GROUP_WIDTH), BF16)
    col_shape = jax.ShapeDtypeStruct((b, s // tq, GROUP_WIDTH, tq), BF16)
    return pl.pallas_call(
        _in_proj_body,
        grid=(b, s // tm),
        in_specs=[
            pl.BlockSpec((1, tm, D_MODEL), lambda bi, i: (bi, i, 0)),
            pl.BlockSpec((1, 1, tm), lambda bi, i: (bi, 0, i)),
            _resident((D_MODEL, N_GROUPS * GROUP_WIDTH)),
            *[_resident((D_MODEL, GROUP_WIDTH))] * len(w_rotated),
            _resident((2, HEAD_DIM // 2, 1)),
        ] + cast_in,
        out_specs=[row_spec, row_spec, row_spec, row_spec, col_spec, row_spec, col_spec] + cast_out,
        out_shape=[row_shape, row_shape, row_shape, row_shape, col_shape, row_shape, col_shape] + cast_shapes,
        compiler_params=pltpu.CompilerParams(
            dimension_semantics=("arbitrary", "arbitrary"), vmem_limit_bytes=VMEM_LIMIT),
        name="in_proj",
    )(x1, pos, w_in, *w_rotated, inv, *cast)


def _retention_body(q_ref, k_ref, v_ref, g_ref, nw_ref, intra_ref, qd_ref, kd_ref, cd_ref, avg_ref, *refs):
    n_cast = (len(refs) - 3) // 2
    o_ref, state_ref, raw_ref = refs[n_cast], refs[-2], refs[-1]
    _cast_rows(refs[:n_cast], refs[n_cast + 1:-2])
    @pl.when(pl.program_id(1) == 0)
    def _():
        state_ref[...] = jnp.zeros(state_ref.shape, F32)

    lane = lax.broadcasted_iota(jnp.int32, (CHUNK, LANES), 1)
    row = lax.broadcasted_iota(jnp.int32, (CHUNK, LANES), 0)
    out_a = lane < HEAD_DIM
    head_a = (lane % HEAD_DIM) < (HEAD_DIM // 2)
    same_head = ((row % HEAD_DIM) < (HEAD_DIM // 2)) == out_a
    avg = avg_ref[...]
    pairs = [slice(p * LANES, (p + 1) * LANES) for p in range(RET_HEADS // 2)]

    states = [state_ref[p] for p in range(len(pairs))]
    for c in range(q_ref.shape[1] // CHUNK):
        rows = slice(c * CHUNK, (c + 1) * CHUNK)
        qs = [q_ref[0, rows, cols] for cols in pairs]
        ks = [k_ref[0, rows, cols] for cols in pairs]
        vs = [v_ref[0, rows, cols] for cols in pairs]
        zero = jnp.zeros_like(qs[0])
        sa = [_dot_nt(jnp.where(head_a, q, zero), k) for q, k in zip(qs, ks)]
        sb = [_dot_nt(jnp.where(head_a, zero, q), k) for q, k in zip(qs, ks)]
        cross = [_dot(q, s.astype(BF16)) for q, s in zip(qs, states)]
        kv = [_dot_tn((k.astype(F32) * kd_ref[p]).astype(BF16), v) for p, (k, v) in enumerate(zip(ks, vs))]
        pa = [(s * intra_ref[2 * p]).astype(BF16) for p, s in enumerate(sa)]
        pb = [(s * intra_ref[2 * p + 1]).astype(BF16) for p, s in enumerate(sb)]
        ia = [_dot(x, v) for x, v in zip(pa, vs)]
        ib = [_dot(x, v) for x, v in zip(pb, vs)]
        for p, cols in enumerate(pairs):
            states[p] = states[p] * cd_ref[p] + jnp.where(same_head, kv[p], 0.0)
            raw_ref[rows, cols] = jnp.where(out_a, ia[p], ib[p]) + cross[p] * qd_ref[p]
    for p in range(len(pairs)):
        state_ref[p] = states[p]

    def head_mean(t):
        hi = t.astype(BF16)
        lo = (t - hi.astype(F32)).astype(BF16)
        return _dot(jnp.concatenate([hi, lo], axis=1), avg)

    for cols in pairs:
        o = raw_ref[:, cols]
        d = o - head_mean(o)
        y = d * lax.rsqrt(head_mean(d * d) + NORM_EPS)
        gate = g_ref[0, :, cols].astype(F32)
        o_ref[0, :, cols] = (gate * jax.nn.sigmoid(gate) * (y * nw_ref[:, cols])).astype(BF16)


def _retention(rq, rk, rv, rg, nw, intra, qd, kd, cd, avg, cast=()):
    b, s, _ = rq.shape
    ts = TS_RET
    cast_in, cast_out, cast_shapes = _cast_specs(cast, b * (s // ts), lambda bi, si: bi * (s // ts) + si)
    seq_spec = pl.BlockSpec((1, ts, RET_WIDTH), lambda bi, si: (bi, si, 0))
    return pl.pallas_call(
        _retention_body,
        grid=(b, s // ts),
        in_specs=[seq_spec, seq_spec, seq_spec, seq_spec,
                  _resident((1, RET_WIDTH)),
                  _resident(intra.shape), _resident(qd.shape), _resident(kd.shape),
                  _resident(cd.shape), _resident(avg.shape)] + cast_in,
        out_specs=[seq_spec] + cast_out,
        out_shape=[jax.ShapeDtypeStruct((b, s, RET_WIDTH), BF16)] + cast_shapes,
        scratch_shapes=[pltpu.VMEM((RET_HEADS // 2, LANES, LANES), F32),
                        pltpu.VMEM((ts, RET_WIDTH), F32)],
        compiler_params=pltpu.CompilerParams(
            dimension_semantics=("arbitrary", "arbitrary"), vmem_limit_bytes=VMEM_LIMIT),
        name="retention",
    )(rq, rk, rv, rg, nw, intra, qd, kd, cd, avg, *cast)


def _diff_attn_body(qt_ref, k_ref, vt_ref, lq1_ref, lk1_ref, lq2_ref, lk2_ref, nw_ref, o_ref,
                    s0_ref, s1_ref, mx0_ref, mx1_ref, m_ref, acc_ref):
    nblk, _, tq = qt_ref.shape[1:]
    s_refs, mx_refs = (s0_ref, s1_ref), (mx0_ref, mx1_ref)
    row = lax.broadcasted_iota(jnp.int32, (LANES, 1), 0)
    first_head = (row % HEAD_DIM) < (HEAD_DIM // 2)
    heads = range(2)
    ones = jnp.ones((ONES_ROWS, tq), BF16)

    def scores(slot, task):
        i, t = task
        qt = qt_ref[0, i]
        zero = jnp.zeros_like(qt)
        k = k_ref[0, pl.ds(pl.multiple_of(t * tq, tq), tq), :]
        for h in heads:
            st = _dot(k, jnp.where(first_head, qt, zero) if h == 0 else jnp.where(first_head, zero, qt))
            s_refs[slot][h] = st
            mx_refs[slot][h] = jnp.max(st, axis=0, keepdims=True)

    def attend(slot, task):
        i, t = task
        vt = jnp.concatenate([vt_ref[0, t], ones], axis=0)
        for h in heads:
            m_old = m_ref[i, h]
            m_new = jnp.maximum(m_old, mx_refs[slot][h])
            p = jnp.exp2(s_refs[slot][h] - m_new)
            acc_ref[i, h] = jnp.exp2(m_old - m_new) * acc_ref[i, h] + _dot(vt, p.astype(BF16))
            m_ref[i, h] = m_new

    half = tq // 2
    early, late = slice(0, half), slice(half, tq)
    causal = (lax.broadcasted_iota(jnp.int32, (half, half), 0)
              <= lax.broadcasted_iota(jnp.int32, (half, half), 1))

    def scores_diagonal(slot, i):
        qt = qt_ref[0, i]
        zero = jnp.zeros_like(qt)
        k_early = k_ref[0, i * tq:i * tq + half, :]
        k_late = k_ref[0, i * tq + half:(i + 1) * tq, :]
        for h in heads:
            qh = jnp.where(first_head, qt, zero) if h == 0 else jnp.where(first_head, zero, qt)
            s_refs[slot][h, early, :] = _dot(k_early, qh)
            s_refs[slot][h, late, late] = _dot(k_late, qh[:, late])

    def attend_diagonal(slot, i):
        vt = jnp.concatenate([vt_ref[0, i], ones], axis=0)
        for h in heads:
            s_ee = jnp.where(causal, s_refs[slot][h, early, early], -jnp.inf)
            s_el = s_refs[slot][h, early, late]
            s_ll = jnp.where(causal, s_refs[slot][h, late, late], -jnp.inf)
            m_e = jnp.max(s_ee, axis=0, keepdims=True)
            m_l = jnp.maximum(jnp.max(s_el, axis=0, keepdims=True), jnp.max(s_ll, axis=0, keepdims=True))
            acc_ref[i, h, :, early] = _dot(vt[:, early], jnp.exp2(s_ee - m_e).astype(BF16))
            acc_ref[i, h, :, late] = (_dot(vt[:, early], jnp.exp2(s_el - m_l).astype(BF16))
                                      + _dot(vt[:, late], jnp.exp2(s_ll - m_l).astype(BF16)))
            m_ref[i, h] = jnp.concatenate([m_e, m_l], axis=1)

    scores_diagonal(0, 0)
    for i in range(nblk):
        if i + 1 < nblk:
            scores_diagonal((i + 1) % 2, i + 1)
        attend_diagonal(i % 2, i)

    def following(task):
        i, t = task
        last = t + 1 == i
        return jnp.where(last, jnp.minimum(i + 1, nblk - 1), i), jnp.where(last, 0, t + 1)

    def plain_block(_, task):
        for u in range(ATT_TASKS_PER_BLOCK):
            nxt = following(task)
            scores(1 - u % 2, nxt)
            attend(u % 2, task)
            task = nxt
        return task

    n_plain = nblk * (nblk - 1) // 2
    assert ATT_TASKS_PER_BLOCK % 2 == 0 and n_plain % ATT_TASKS_PER_BLOCK == 0
    first = (jnp.int32(1), jnp.int32(0))
    scores(0, first)
    lax.fori_loop(0, n_plain // ATT_TASKS_PER_BLOCK, plain_block, first)

    lam = (jnp.exp(jnp.sum(lq1_ref[...] * lk1_ref[...], axis=-1, keepdims=True))
           - jnp.exp(jnp.sum(lq2_ref[...] * lk2_ref[...], axis=-1, keepdims=True)) + LAMBDA_INIT)

    def finish_tile(i, carry):
        num = [acc_ref[i, h, 0:LANES, :] for h in heads]
        den = [acc_ref[i, h, LANES:LANES + 1, :] for h in heads]
        o = num[0] / den[0] - lam * (num[1] / den[1])
        o = o * lax.rsqrt(jnp.mean(o * o, axis=0, keepdims=True) + NORM_EPS)
        o = (o * (nw_ref[...] * (1.0 - LAMBDA_INIT))).astype(BF16)
        o_ref[0, pl.ds(pl.multiple_of(i * tq, tq), tq), :] = o.T
        return carry

    lax.fori_loop(0, nblk, finish_tile, 0, unroll=True)


def _diff_attn(dqt, dk, dvt, lam_vecs, nw):
    b, nblk, _, tq = dqt.shape
    s = nblk * tq
    return pl.pallas_call(
        _diff_attn_body,
        grid=(b, DIFF_HEADS),
        in_specs=[
            pl.BlockSpec((1, nblk, LANES, tq), lambda bi, j: (bi, 0, j, 0)),
            pl.BlockSpec((1, s, LANES), lambda bi, j: (bi, 0, j)),
            pl.BlockSpec((1, nblk, LANES, tq), lambda bi, j: (bi, 0, j, 0)),
            *[_resident((1, HEAD_DIM))] * len(lam_vecs),
            pl.BlockSpec((LANES, 1), lambda bi, j: (j, 0)),
        ],
        out_specs=pl.BlockSpec((1, s, LANES), lambda bi, j: (bi, 0, j)),
        out_shape=jax.ShapeDtypeStruct((b, s, DIFF_WIDTH), BF16),
        scratch_shapes=[pltpu.VMEM((2, tq, tq), F32)] * 2
                       + [pltpu.VMEM((2, 1, tq), F32)] * 2
                       + [pltpu.VMEM((nblk, 2, 1, tq), F32),
                          pltpu.VMEM((nblk, 2, LANES + ONES_ROWS, tq), F32)],
        compiler_params=pltpu.CompilerParams(
            dimension_semantics=("arbitrary", "arbitrary"), vmem_limit_bytes=VMEM_LIMIT),
        name="diff_attn",
    )(dqt, dk, dvt, *lam_vecs, nw)


def _rotation_inv_freqs():
    half = HEAD_DIM // 2
    f32 = np.float32
    inv_ret = f32(1.0) / np.power(f32(RET_THETA), np.linspace(0.0, 1.0, half, dtype=f32))
    inv_rope = f32(1.0) / np.power(f32(ROPE_THETA), np.arange(0, HEAD_DIM, 2, dtype=f32) / f32(HEAD_DIM))
    return np.stack([inv_ret, inv_rope]).astype(f32)[:, :, None]


def _retention_tables():
    h, f32 = RET_HEADS, np.float32
    log_g = np.log(f32(1.0) - np.power(f32(2.0), f32(-5.0) - np.arange(h, dtype=f32))).astype(f32)
    idx = np.arange(CHUNK, dtype=f32)
    rel = idx[:, None] - idx[None, :]
    intra = np.where(rel >= 0, np.exp(log_g[:, None, None] * np.maximum(rel, f32(0.0))), f32(0.0)).astype(f32)
    q_decay = np.exp(log_g[:, None] * (idx + f32(1.0))).astype(f32)
    k_decay = np.exp(log_g[:, None] * (f32(CHUNK - 1.0) - idx)).astype(f32)
    chunk_decay = np.exp(log_g * f32(CHUNK)).astype(f32)

    value_head = np.arange(LANES) // HEAD_DIM
    qk_head = (np.arange(LANES) % HEAD_DIM) // (HEAD_DIM // 2)

    def slab(t, head_of_lane):
        return np.ascontiguousarray(t.reshape(h // 2, 2, CHUNK)[:, head_of_lane, :].transpose(0, 2, 1))

    cd = chunk_decay.reshape(h // 2, 2)[:, qk_head]
    cd = np.ascontiguousarray(np.broadcast_to(cd[:, :, None], (h // 2, LANES, LANES)))
    avg = (value_head[:, None] == value_head[None, :]).astype(f32) / f32(HEAD_DIM)
    return intra, slab(q_decay, value_head), slab(k_decay, qk_head), cd, jnp.asarray(np.concatenate([avg, avg]), BF16)


def _slab_perm(w, even_odd):
    k, half, slabs = w.shape[0], HEAD_DIM // 2, GROUP_WIDTH // LANES
    if even_odd:
        t = w.reshape(k, slabs, 2, half, 2).transpose(0, 1, 4, 2, 3)
    else:
        t = w.reshape(k, slabs, 2, 2, half).transpose(0, 1, 3, 2, 4)
    return t.reshape(k, GROUP_WIDTH)


def kernel(x, positions, ffn1_w_gate, ffn1_w_up, ffn1_w_down, ln1_w, ln1_b, w_in, ret_norm_w,
           diff_lambda_q1, diff_lambda_k1, diff_lambda_q2, diff_lambda_k2, diff_norm_w, w_out,
           ln2_w, ln2_b, ffn2_w_gate, ffn2_w_up, ffn2_w_down, ln3_w, ln3_b):
    b, s, d = x.shape
    n = b * s
    l = 0
    xf = x.reshape(n, d)
    pos = positions.reshape(b, 1, s)

    inv = _rotation_inv_freqs()
    intra, qd, kd, cd, avg = _retention_tables()
    lam_vecs = (diff_lambda_q1, diff_lambda_k1, diff_lambda_q2, diff_lambda_k2)

    x1, w_in_bf16 = _ffn_ln(xf, ffn1_w_gate[l].astype(BF16), ffn1_w_up[l].astype(BF16), ffn1_w_down[l].astype(BF16),
                            ln1_w[l][None], ln1_b[l][None], cast=(w_in,))
    w_rotated = [_slab_perm(w_in_bf16[:, g * GROUP_WIDTH:(g + 1) * GROUP_WIDTH], even_odd)
                 for g, even_odd in ((0, True), (1, True), (4, False), (5, False))]
    rq, rk, rv, rg, dqt, dk, dvt, wg2, wu2, wo = _in_proj(
        x1.reshape(b, s, d), pos, w_in_bf16, w_rotated, inv, cast=(ffn2_w_gate, ffn2_w_up, w_out))
    ret, wd2 = _retention(rq, rk, rv, rg, ret_norm_w[l][None], intra, qd, kd, cd, avg, cast=(ffn2_w_down,))
    dif = _diff_attn(dqt, dk, dvt, lam_vecs, diff_norm_w[l][:, None])
    mixer = (ret.reshape(n, RET_WIDTH), dif.reshape(n, DIFF_WIDTH), wo, ln2_w[l][None], ln2_b[l][None])
    (out,) = _ffn_ln(x1, wg2, wu2, wd2, ln3_w[l][None], ln3_b[l][None], mixer=mixer)
    return out.reshape(b, s, d)
```
